```python
import math
import jax
import jax.numpy as jnp
from jax import lax
import numpy as np

D_MODEL = 2048
BATCH = 4
SEQ = 4096
DEPTH = 4

GRID_W = 64
CTX_LEN = 256
EPS = 1e-6
N_MIXERS = 3
MIXER_READS_CTX = (False, True, True)
N_CONV_LAYERS = len(range(0, DEPTH, N_MIXERS))
N_SSM_LAYERS = len(range(1, DEPTH, N_MIXERS))
N_ATTN_LAYERS = len(range(2, DEPTH, N_MIXERS))
CONV_WIDTH = 3
SSM_GROUP = 16
SSM_GROUPS = D_MODEL // SSM_GROUP
SSM_STATE = 64
DT_MIN = 1e-3
DT_MAX = 1e-1
N_HEADS = 16
HEAD_DIM = D_MODEL // (2 * N_HEADS)
Q_BLOCK = 128
ROPE_THETA = 10000.0
N_EXPERTS = 16
D_EXPERT = D_MODEL // 2
CAPACITY_FACTOR = 2

kernel_name = 'hybrid_conv_s5_diffattn_ecmoe_dit'


def rms_norm(x, g):
    xf = x.astype(jnp.float32)
    y = xf * lax.rsqrt(jnp.mean(xf * xf, axis=-1, keepdims=True) + EPS)
    return (y * g.astype(jnp.float32)).astype(x.dtype)


def modulate(h, shift, scale):
    return h * (1.0 + scale) + shift


def short_conv_mixer(h, w_in, w_conv, w_out):
    n = h.shape[1]
    gate_b, gate_c, v = jnp.split(h @ w_in, 3, axis=-1)
    pad = CONV_WIDTH // 2
    z = jnp.pad(gate_c * v, ((0, 0), (pad, pad), (0, 0)))
    conv = sum(w_conv[j] * z[:, j:j + n] for j in range(CONV_WIDTH))
    return (gate_b * conv) @ w_out


def _cmul(ar, ai, br, bi):
    return ar * br - ai * bi, ar * bi + ai * br


def _linear_recurrence_combine(e1, e2):
    a1r, a1i, b1r, b1i = e1
    a2r, a2i, b2r, b2i = e2
    ar, ai = _cmul(a2r, a2i, a1r, a1i)
    br, bi = _cmul(a2r, a2i, b1r, b1i)
    return ar, ai, br + b2r, bi + b2i


def _s5_direction(u_lat, u_ctx, a_re, a_im, log_dt, b_re, b_im, c_re, c_im, reverse, ctx_out):
    f32 = jnp.float32
    a_re, a_im = a_re.astype(f32), a_im.astype(f32)
    dt = jnp.exp(log_dt.astype(f32))[:, None]
    mag = jnp.exp(dt * a_re)
    abar_r, abar_i = mag * jnp.cos(dt * a_im), mag * jnp.sin(dt * a_im)
    den = a_re * a_re + a_im * a_im
    num_r, num_i = _cmul(abar_r - 1.0, abar_i, a_re, -a_im)
    bb_r, bb_i = _cmul((num_r / den)[..., None], (num_i / den)[..., None],
                       b_re.astype(f32), b_im.astype(f32))

    def scan(u):
        n = u.shape[0]
        a_r = jnp.broadcast_to(abar_r, (n, 1) + abar_r.shape)
        a_i = jnp.broadcast_to(abar_i, (n, 1) + abar_i.shape)
        bu_r = jnp.einsum('lbgc,gpc->lbgp', u, bb_r)
        bu_i = jnp.einsum('lbgc,gpc->lbgp', u, bb_i)
        return lax.associative_scan(_linear_recurrence_combine, (a_r, a_i, bu_r, bu_i),
                                    reverse=reverse, axis=0)

    def readout(s_r, s_i):
        return (jnp.einsum('lbgp,gcp->lbgc', s_r, c_re.astype(f32))
                - jnp.einsum('lbgp,gcp->lbgc', s_i, c_im.astype(f32)))

    _, _, sc_r, sc_i = scan(u_ctx)
    end = 0 if reverse else -1
    s0_r, s0_i = sc_r[end], sc_i[end]
    ap_r, ap_i, sl_r, sl_i = scan(u_lat)
    carry_r, carry_i = _cmul(ap_r, ap_i, s0_r, s0_i)
    y_lat = readout(sl_r + carry_r, sl_i + carry_i)
    y_ctx = readout(sc_r, sc_i) if ctx_out else None
    return y_lat, y_ctx


def _to_groups(u):
    b, n, _ = u.shape
    return u.astype(jnp.float32).reshape(b, n, SSM_GROUPS, SSM_GROUP).transpose(1, 0, 2, 3)


def _gelu_glu(y, w_glu, dtype):
    n, b = y.shape[:2]
    z = jax.nn.gelu(y.transpose(1, 0, 2, 3).reshape(b, n, D_MODEL)).astype(dtype)
    val, gate = jnp.split(z @ w_glu, 2, axis=-1)
    return val * jax.nn.sigmoid(gate)


def s5_mixer(hx, hc, w_in, a_re, a_im, log_dt, b_re, b_im, c_re, c_im, d_skip, w_glu, ctx_out):
    d = d_skip.astype(jnp.float32).reshape(SSM_GROUPS, SSM_GROUP)
    u_lat = _to_groups(hx @ w_in)
    u_ctx = _to_groups(hc @ w_in)
    y_lat = d * u_lat
    y_ctx = d * u_ctx if ctx_out else None
    for direction in range(2):
        yl, yc = _s5_direction(u_lat, u_ctx, a_re[direction], a_im[direction], log_dt[direction],
                               b_re[direction], b_im[direction], c_re[direction], c_im[direction],
                               reverse=direction == 1, ctx_out=ctx_out)
        y_lat = y_lat + yl
        if ctx_out:
            y_ctx = y_ctx + yc
    out_lat = _gelu_glu(y_lat, w_glu, hx.dtype)
    out_ctx = _gelu_glu(y_ctx, w_glu, hc.dtype) if ctx_out else None
    return out_lat, out_ctx


def axial_rope_tables(n_tokens):
    rows = n_tokens // GRID_W
    row = jnp.broadcast_to(jnp.arange(rows, dtype=jnp.float32)[:, None], (rows, GRID_W)).reshape(-1)
    col = jnp.broadcast_to(jnp.arange(GRID_W, dtype=jnp.float32)[None, :], (rows, GRID_W)).reshape(-1)
    n_freq = HEAD_DIM // 4
    inv_freq = ROPE_THETA ** (-jnp.arange(n_freq, dtype=jnp.float32) / n_freq)
    ang = jnp.stack([row[:, None] * inv_freq, col[:, None] * inv_freq], axis=1)
    return jnp.cos(ang), jnp.sin(ang)


def apply_axial_rope(q, cos, sin):
    b, n, h, c, _ = q.shape
    qr = q.astype(jnp.float32).reshape(b, n, h, c, 2, 2, HEAD_DIM // 4)
    x1, x2 = qr[..., 0, :], qr[..., 1, :]
    cs, sn = cos[None, :, None, None], sin[None, :, None, None]
    out = jnp.stack([x1 * cs - x2 * sn, x2 * cs + x1 * sn], axis=-2)
    return out.reshape(q.shape).astype(q.dtype)


def _diff_attend(q, k, v, lam):
    s = jnp.einsum('bqhcd,bkhcd->bhcqk', q, k).astype(jnp.float32) * (HEAD_DIM ** -0.5)
    p = jax.nn.softmax(s, axis=-1)
    w = p[:, :, 0] - lam * p[:, :, 1]
    return jnp.einsum('bhqk,bkhd->bqhd', w.astype(v.dtype), v)


def diff_attention_mixer(hx, hc, w_qkv, lam_params, subln_g, w_o, cos, sin, lambda_init, ctx_out):
    b, n, _ = hx.shape

    def qk_heads(t):
        return t.reshape(t.shape[0], t.shape[1], N_HEADS, 2, HEAD_DIM)

    def v_heads(t):
        return t.reshape(t.shape[0], t.shape[1], N_HEADS, 2 * HEAD_DIM)

    lp = lam_params.astype(jnp.float32)
    lam = jnp.exp(jnp.sum(lp[0] * lp[1])) - jnp.exp(jnp.sum(lp[2] * lp[3])) + lambda_init
    qx, kx, vx = jnp.split(hx @ w_qkv, 3, axis=-1)
    qx = apply_axial_rope(qk_heads(qx), cos, sin)
    kx = apply_axial_rope(qk_heads(kx), cos, sin)
    kc, vc = jnp.split(hc @ w_qkv[:, D_MODEL:], 2, axis=-1)
    kc, vc = qk_heads(kc), v_heads(vc)
    k_all = jnp.concatenate([kx, kc], axis=1)
    v_all = jnp.concatenate([v_heads(vx), vc], axis=1)
    nb = n // Q_BLOCK
    q_blocks = qx.reshape(b, nb, Q_BLOCK, N_HEADS, 2, HEAD_DIM).transpose(1, 0, 2, 3, 4, 5)
    o_blocks = lax.map(lambda qb: _diff_attend(qb, k_all, v_all, lam), q_blocks)
    ox = o_blocks.transpose(1, 0, 2, 3, 4).reshape(b, n, N_HEADS, 2 * HEAD_DIM)

    def finish(o):
        o = rms_norm(o, subln_g) * (1.0 - lambda_init)
        return o.reshape(o.shape[0], o.shape[1], D_MODEL) @ w_o

    yx = finish(ox)
    yc = None
    if ctx_out:
        qc = qk_heads(hc @ w_qkv[:, :D_MODEL])
        yc = finish(_diff_attend(qc, kc, vc, lam))
    return yx, yc


def expert_choice_moe(h, router, w_gate, w_up, w_down):
    b, n, d = h.shape
    cap = max(1, CAPACITY_FACTOR * n // N_EXPERTS)
    affinity = jax.nn.softmax((h @ router).astype(jnp.float32), axis=-1)
    g, idx = lax.top_k(jnp.swapaxes(affinity, 1, 2), cap)
    xg = jax.vmap(lambda hb, ib: hb[ib])(h, idx)
    act = (jax.nn.silu(jnp.einsum('becd,edf->becf', xg, w_gate))
           * jnp.einsum('becd,edf->becf', xg, w_up))
    y = jnp.einsum('becf,efd->becd', act, w_down) * g[..., None].astype(h.dtype)
    return jax.vmap(lambda yb, ib: jnp.zeros((n, d), yb.dtype).at[ib.reshape(-1)].add(yb.reshape(-1, d)))(y, idx)


def setup_inputs(seed: int = 0) -> dict:
    key = jax.random.key(seed)
    ks = iter(jax.random.split(key, 40))
    f32 = jnp.float32

    def nrm(shape, scale):
        return jax.random.normal(next(ks), shape, f32) * scale

    def gain(shape):
        return 1.0 + nrm(shape, 0.01)

    D, E, F = D_MODEL, N_EXPERTS, D_EXPERT
    G, P, HG = SSM_GROUPS, SSM_STATE, SSM_GROUP
    n_idx = jnp.arange(P, dtype=f32)
    return {
        'x': nrm((BATCH, SEQ, D), 1.0),
        'c': nrm((BATCH, D), 1.0),
        'ctx': nrm((BATCH, CTX_LEN, D), 1.0),
        'c_ctx': nrm((D,), 1.0),
        'ada_w': nrm((DEPTH, D, 6 * D), D ** -0.5),
        'ada_b': nrm((DEPTH, 6 * D), 0.02),
        'norm_mix_g': gain((DEPTH, D)),
        'norm_ffn_g': gain((DEPTH, D)),
        'final_norm_g': gain((D,)),
        'conv_w_in': nrm((N_CONV_LAYERS, D, 3 * D), D ** -0.5),
        'conv_w': nrm((N_CONV_LAYERS, CONV_WIDTH, D), CONV_WIDTH ** -0.5),
        'conv_w_out': nrm((N_CONV_LAYERS, D, D), D ** -0.5),
        'ssm_w_in': nrm((N_SSM_LAYERS, D, D), D ** -0.5),
        'ssm_a_re': -0.5 + nrm((N_SSM_LAYERS, 2, G, P), 0.01),
        'ssm_a_im': math.pi * n_idx + nrm((N_SSM_LAYERS, 2, G, P), 0.01),
        'ssm_log_dt': jax.random.uniform(next(ks), (N_SSM_LAYERS, 2, G), f32,
                                         math.log(DT_MIN), math.log(DT_MAX)),
        'ssm_b_re': nrm((N_SSM_LAYERS, 2, G, P, HG), (2 * HG) ** -0.5),
        'ssm_b_im': nrm((N_SSM_LAYERS, 2, G, P, HG), (2 * HG) ** -0.5),
        'ssm_c_re': nrm((N_SSM_LAYERS, 2, G, HG, P), P ** -0.5),
        'ssm_c_im': nrm((N_SSM_LAYERS, 2, G, HG, P), P ** -0.5),
        'ssm_d': nrm((N_SSM_LAYERS, D), 1.0),
        'ssm_w_glu': nrm((N_SSM_LAYERS, D, 2 * D), D ** -0.5),
        'attn_w_qkv': nrm((N_ATTN_LAYERS, D, 3 * D), D ** -0.5),
        'attn_lambda': nrm((N_ATTN_LAYERS, 4, HEAD_DIM), 0.1),
        'attn_subln_g': gain((N_ATTN_LAYERS, 2 * HEAD_DIM)),
        'attn_w_o': nrm((N_ATTN_LAYERS, D, D), D ** -0.5),
        'moe_router': nrm((DEPTH, D, E), D ** -0.5),
        'moe_w_gate': nrm((DEPTH, E, D, F), D ** -0.5),
        'moe_w_up': nrm((DEPTH, E, D, F), D ** -0.5),
        'moe_w_down': nrm((DEPTH, E, F, D), F ** -0.5),
    }


def reference(x, c, ctx, c_ctx, ada_w, ada_b, norm_mix_g, norm_ffn_g, final_norm_g,
              conv_w_in, conv_w, conv_w_out,
              ssm_w_in, ssm_a_re, ssm_a_im, ssm_log_dt, ssm_b_re, ssm_b_im, ssm_c_re, ssm_c_im,
              ssm_d, ssm_w_glu,
              attn_w_qkv, attn_lambda, attn_subln_g, attn_w_o,
              moe_router, moe_w_gate, moe_w_up, moe_w_down):
    cos, sin = axial_rope_tables(x.shape[1])
    silu_c = jax.nn.silu(c)
    silu_cc = jax.nn.silu(c_ctx)
    ctx_live = [any(MIXER_READS_CTX[j % N_MIXERS] for j in range(i + 1, DEPTH)) for i in range(DEPTH)]
    for i in range(DEPTH):
        kind, slot = i % N_MIXERS, i // N_MIXERS
        ctx_out = ctx_live[i]
        mx = jnp.split((silu_c @ ada_w[i] + ada_b[i])[:, None, :], 6, axis=-1)
        hx = modulate(rms_norm(x, norm_mix_g[i]), mx[0], mx[1])
        if ctx_out or MIXER_READS_CTX[kind]:
            mc = jnp.split((silu_cc @ ada_w[i] + ada_b[i])[None, None, :], 6, axis=-1)
            hc = modulate(rms_norm(ctx, norm_mix_g[i]), mc[0], mc[1])
        if kind == 0:
            yx = short_conv_mixer(hx, conv_w_in[slot], conv_w[slot], conv_w_out[slot])
            yc = short_conv_mixer(hc, conv_w_in[slot], conv_w[slot], conv_w_out[slot]) if ctx_out else None
        elif kind == 1:
            yx, yc = s5_mixer(hx, hc, ssm_w_in[slot], ssm_a_re[slot], ssm_a_im[slot], ssm_log_dt[slot],
                              ssm_b_re[slot], ssm_b_im[slot], ssm_c_re[slot], ssm_c_im[slot],
                              ssm_d[slot], ssm_w_glu[slot], ctx_out)
        else:
            yx, yc = diff_attention_mixer(hx, hc, attn_w_qkv[slot], attn_lambda[slot], attn_subln_g[slot],
                                          attn_w_o[slot], cos, sin,
                                          0.8 - 0.6 * math.exp(-0.3 * i), ctx_out)
        x = x + mx[2] * yx
        hf = modulate(rms_norm(x, norm_ffn_g[i]), mx[3], mx[4])
        x = x + mx[5] * expert_choice_moe(hf, moe_router[i], moe_w_gate[i], moe_w_up[i], moe_w_down[i])
        if ctx_out:
            ctx = ctx + mc[2] * yc
            hfc = modulate(rms_norm(ctx, norm_ffn_g[i]), mc[3], mc[4])
            ctx = ctx + mc[5] * expert_choice_moe(hfc, moe_router[i], moe_w_gate[i], moe_w_up[i], moe_w_down[i])
    return rms_norm(x, final_norm_g)
```

```python
import functools
import math

import jax
import jax.numpy as jnp
from jax import lax
from jax.experimental import pallas as pl
from jax.experimental.pallas import tpu as pltpu

F32 = jnp.float32
BF16 = jnp.bfloat16
EPS = 1e-6
N_MIXERS = 3
MIXER_READS_CTX = (False, True, True)
GRID_W = 64
ROPE_THETA = 10000.0
CAPACITY_FACTOR = 2
SSM_CHUNK = 16
SUBLANES = 8
LANES = 128
VMEM_LIMIT = 56 * 1024 * 1024


def _cparams(*sem):
    return pltpu.CompilerParams(dimension_semantics=sem, vmem_limit_bytes=VMEM_LIMIT)


def _pick(n, cands):
    for c in cands:
        if n % c == 0:
            return c
    raise ValueError(f"no tile in {cands} divides {n}")


class Geo:
    def __init__(self, B, L, Lc, D):
        self.B, self.L, self.Lc, self.D = B, L, Lc, D
        self.RL = B * L
        self.Rv = B * (L + Lc)
        self.R = -(-self.Rv // L) * L
        self.tm = _pick(math.gcd(L, B * Lc), (512, 256, 128, 64, 32, 16, 8))
        self.tn = _pick(D, (512, 256, 128))

    def mod_row(self, i):
        return jnp.minimum((i * self.tm) // self.L, self.B)

    def tiles(self, with_ctx):
        return (self.Rv if with_ctx else self.RL) // self.tm


def _ada_body(c_ref, w_ref, b_ref, o_ref):
    c = c_ref[...]
    s = (c * jax.nn.sigmoid(c)).astype(BF16)
    o_ref[...] = jnp.dot(s, w_ref[...].astype(BF16), preferred_element_type=F32) + b_ref[...]


def ada_modulation(c_all, ada_w, ada_b):
    depth, D, N = ada_w.shape
    tn = _pick(N, (1024, 512, 256, 128))
    return pl.pallas_call(
        _ada_body,
        out_shape=jax.ShapeDtypeStruct((depth, SUBLANES, N), F32),
        grid=(depth, N // tn),
        in_specs=[pl.BlockSpec((SUBLANES, D), lambda l, j: (0, 0)),
                  pl.BlockSpec((None, D, tn), lambda l, j: (l, 0, j)),
                  pl.BlockSpec((None, 1, tn), lambda l, j: (l, 0, j))],
        out_specs=pl.BlockSpec((None, SUBLANES, tn), lambda l, j: (l, 0, j)),
        compiler_params=_cparams("parallel", "parallel"),
        name="ada_modulation",
    )(c_all, ada_w, ada_b.reshape(depth, 1, N))


def _mm_body(*refs, n_lhs, n_w, n_epi, prologue, epilogue):
    lhs = refs[:n_lhs]
    ws = refs[n_lhs:n_lhs + n_w]
    epi = refs[n_lhs + n_w:n_lhs + n_w + n_epi]
    outs = refs[n_lhs + n_w + n_epi:-1]
    hb = refs[-1]

    @pl.when(pl.program_id(1) == 0)
    def _():
        hb[...] = prologue(*lhs).astype(hb.dtype)

    h = hb[...]
    accs = [jnp.dot(h, w[...], preferred_element_type=F32) for w in ws]
    for o, v in zip(outs, epilogue(accs, *epi)):
        o[...] = v.astype(o.dtype)


def fused_matmul(name, geo, n_tiles, lhs, w, n_split, epi, prologue, epilogue, outs, alias=None):
    K = w.shape[0]
    N = w.shape[1] // n_split
    tm, tn = geo.tm, geo.tn
    nj = N // tn
    w_specs = [pl.BlockSpec((K, tn), functools.partial(lambda i, j, s: (0, s * nj + j), s=s))
               for s in range(n_split)]
    arrays = [a for a, _ in lhs] + [w] * n_split + [a for a, _ in epi]
    specs = [s for _, s in lhs] + w_specs + [s for _, s in epi]
    body = functools.partial(_mm_body, n_lhs=len(lhs), n_w=n_split, n_epi=len(epi),
                             prologue=prologue, epilogue=epilogue)
    res = pl.pallas_call(
        body,
        out_shape=[jax.ShapeDtypeStruct((geo.R, N), dt) for dt in outs],
        grid=(n_tiles, nj),
        in_specs=specs,
        out_specs=[pl.BlockSpec((tm, tn), lambda i, j: (i, j)) for _ in outs],
        scratch_shapes=[pltpu.VMEM((tm, K), BF16)],
        input_output_aliases=alias or {},
        compiler_params=_cparams("parallel", "arbitrary"),
        name=name,
    )(*arrays)
    return res


def _row_full(geo, K):
    return pl.BlockSpec((geo.tm, K), lambda i, j: (i, 0))


def _row_tile(geo):
    return pl.BlockSpec((geo.tm, geo.tn), lambda i, j: (i, j))


def _vec_full(K):
    return pl.BlockSpec((1, K), lambda i, j: (0, 0))


def _mod_full(geo, layer, k):
    return pl.BlockSpec((None, None, None, 1, geo.D),
                        lambda i, j: (layer, geo.mod_row(i), k, 0, 0))


def _mod_tile(geo, layer, k):
    return pl.BlockSpec((None, None, None, 1, geo.tn),
                        lambda i, j: (layer, geo.mod_row(i), k, 0, j))


def _norm_mod(x_ref, g_ref, sh_ref, sc_ref):
    x = x_ref[...]
    y = x * lax.rsqrt(jnp.mean(x * x, axis=-1, keepdims=True) + EPS)
    return (y * g_ref[...]) * (1.0 + sc_ref[...]) + sh_ref[...]


def _norm_mod_inputs(geo, xc, gain, modp, layer, k_shift):
    return [(xc, _row_full(geo, geo.D)), (gain.reshape(1, geo.D), _vec_full(geo.D)),
            (modp, _mod_full(geo, layer, k_shift)), (modp, _mod_full(geo, layer, k_shift + 1))]


def _residual_epilogue(accs, x_ref, gate_ref):
    return [x_ref[...] + gate_ref[...] * accs[0]]


def _conv_in_epilogue(accs):
    gate_b, gate_c, v = accs
    return [gate_b, gate_c * v]


def _conv_prologue(gb_ref, z_ref, zp_ref, zn_ref, wc_ref, *, geo):
    tm = geo.tm
    z = z_ref[...]
    t = lax.broadcasted_iota(jnp.int32, (tm, 1), 0)
    row = pl.program_id(0) * tm + t
    is_lat = row < geo.RL
    pos = jnp.where(is_lat, row % geo.L, (row - geo.RL) % geo.Lc)
    first = pos == 0
    last = pos == jnp.where(is_lat, geo.L - 1, geo.Lc - 1)
    z_prev = jnp.where(t == 0, zp_ref[SUBLANES - 1:SUBLANES, :], pltpu.roll(z, 1, axis=0))
    z_prev = jnp.where(first, 0.0, z_prev)
    z_next = jnp.where(t == tm - 1, zn_ref[0:1, :], pltpu.roll(z, tm - 1, axis=0))
    z_next = jnp.where(last, 0.0, z_next)
    wc = wc_ref[...]
    conv = wc[0:1, :] * z_prev + wc[1:2, :] * z + wc[2:3, :] * z_next
    return gb_ref[...] * conv


def conv_mixer(geo, xc, modp, layer, gain, w_in, w_conv, w_out, with_ctx):
    D, tm = geo.D, geo.tm
    n_tiles = geo.tiles(with_ctx)
    gb, z = fused_matmul("conv_in", geo, n_tiles,
                         _norm_mod_inputs(geo, xc, gain, modp, layer, 0),
                         w_in.astype(BF16), 3, [], _norm_mod, _conv_in_epilogue, [F32, F32])
    r8 = tm // SUBLANES
    last_blk = geo.R // SUBLANES - 1
    halo_prev = pl.BlockSpec((SUBLANES, D), lambda i, j: (jnp.maximum(i * r8 - 1, 0), 0))
    halo_next = pl.BlockSpec((SUBLANES, D), lambda i, j: (jnp.minimum((i + 1) * r8, last_blk), 0))
    lhs = [(gb, _row_full(geo, D)), (z, _row_full(geo, D)), (z, halo_prev), (z, halo_next),
           (w_conv, pl.BlockSpec((3, D), lambda i, j: (0, 0)))]
    epi = [(xc, _row_tile(geo)), (modp, _mod_tile(geo, layer, 2))]
    (out,) = fused_matmul("conv_out", geo, n_tiles, lhs, w_out.astype(BF16), 1, epi,
                          functools.partial(_conv_prologue, geo=geo), _residual_epilogue, [F32],
                          alias={len(lhs) + 1: 0})
    return out


def _s5_operators(a_re, a_im, log_dt, b_re, b_im, c_re, c_im, T):
    G, P = a_re.shape[1:]
    Cg = b_re.shape[-1]
    k = jnp.arange(T + 1, dtype=F32)[:, None, None]
    mt = 0.0
    bst, cst, aT = [], [], []
    lag = jnp.arange(T)[:, None] - jnp.arange(T)[None, :]
    for d in range(2):
        dt = jnp.exp(log_dt[d])[:, None]
        mag = jnp.exp(k * dt * a_re[d])
        pw_r, pw_i = mag * jnp.cos(k * dt * a_im[d]), mag * jnp.sin(k * dt * a_im[d])
        den = a_re[d] * a_re[d] + a_im[d] * a_im[d]
        nr = (pw_r[1] - 1.0) * a_re[d] + pw_i[1] * a_im[d]
        ni = pw_i[1] * a_re[d] - (pw_r[1] - 1.0) * a_im[d]
        fr, fi = (nr / den)[..., None], (ni / den)[..., None]
        bb_r = fr * b_re[d] - fi * b_im[d]
        bb_i = fr * b_im[d] + fi * b_re[d]
        w_r = c_re[d][None] * pw_r[:T, :, None, :] - c_im[d][None] * pw_i[:T, :, None, :]
        w_i = c_re[d][None] * pw_i[:T, :, None, :] + c_im[d][None] * pw_r[:T, :, None, :]
        kk = (jnp.einsum('kgjp,gpc->gkjc', w_r, bb_r, precision='highest')
              - jnp.einsum('kgjp,gpc->gkjc', w_i, bb_i, precision='highest'))
        dlag = lag if d == 0 else -lag
        m = jnp.where((dlag >= 0)[None, :, :, None, None], kk[:, jnp.clip(dlag, 0, T - 1)], 0.0)
        mt = mt + m.transpose(0, 2, 4, 1, 3).reshape(G, T * Cg, T * Cg)
        e = jnp.arange(T - 1, -1, -1) if d == 0 else jnp.arange(T)
        er, ei = pw_r[e], pw_i[e]
        s_r = er[:, :, :, None] * bb_r[None] - ei[:, :, :, None] * bb_i[None]
        s_i = er[:, :, :, None] * bb_i[None] + ei[:, :, :, None] * bb_r[None]
        bst.append(tuple(v.transpose(1, 0, 3, 2).reshape(G, T * Cg, P) for v in (s_r, s_i)))
        e = jnp.arange(1, T + 1) if d == 0 else jnp.arange(T, 0, -1)
        er, ei = pw_r[e], pw_i[e]
        o_r = c_re[d][None] * er[:, :, None, :] - c_im[d][None] * ei[:, :, None, :]
        o_i = -(c_re[d][None] * ei[:, :, None, :] + c_im[d][None] * er[:, :, None, :])
        cst.append(tuple(v.transpose(1, 3, 0, 2).reshape(G, P, T * Cg) for v in (o_r, o_i)))
        aT.append((pw_r[T], pw_i[T]))
    return mt, bst, cst, aT


def _pair_blockdiag(m):
    G, a, b = m.shape
    m = m.reshape(G // 2, 2, a, b)
    z = jnp.zeros_like(m[:, 0])
    return jnp.concatenate([jnp.concatenate([m[:, 0], z], axis=2),
                            jnp.concatenate([z, m[:, 1]], axis=2)], axis=1)


def _s5_state_in_body(u_ref, w_ref, ds_ref):
    ucat = jnp.concatenate([u_ref[0], u_ref[1]], axis=1).astype(BF16)
    for d in range(2):
        for part in range(2):
            ds_ref[d, part] = jnp.dot(ucat, w_ref[d, part], preferred_element_type=F32)


def _s5_scan_body(ds_ref, a_ref, s_ref, *, n_ctx_tiles, n_tiles, B):
    q = SUBLANES // B
    a_re, a_im = a_ref[0], a_ref[1]
    zero = jnp.zeros((B, ds_ref.shape[-1]), F32)

    def run(lo, hi, reverse, state):
        def step(i, st):
            s_re, s_im = st
            tile = (hi - 1 - i) if reverse else (lo + i)
            r0 = pl.multiple_of(tile * SUBLANES, SUBLANES)
            d_re = ds_ref[0, pl.ds(r0, SUBLANES), :]
            d_im = ds_ref[1, pl.ds(r0, SUBLANES), :]
            o_re, o_im = [None] * q, [None] * q
            for k in (range(q - 1, -1, -1) if reverse else range(q)):
                o_re[k], o_im[k] = s_re, s_im
                dk_re, dk_im = d_re[k * B:(k + 1) * B], d_im[k * B:(k + 1) * B]
                s_re, s_im = (a_re * s_re - a_im * s_im + dk_re, a_re * s_im + a_im * s_re + dk_im)
            s_ref[0, pl.ds(r0, SUBLANES), :] = jnp.concatenate(o_re, axis=0)
            s_ref[1, pl.ds(r0, SUBLANES), :] = jnp.concatenate(o_im, axis=0)
            return s_re, s_im
        return lax.fori_loop(0, hi - lo, step, state)

    @pl.when(pl.program_id(0) == 0)
    def _():
        run(0, n_tiles, False, (zero, zero))

    @pl.when(pl.program_id(0) == 1)
    def _():
        st = run(0, n_ctx_tiles, True, (zero, zero))
        run(n_ctx_tiles, n_tiles, True, st)


def _s5_out_body(u_ref, mt_ref, d_ref, s_ref, w_ref, y_ref):
    ys = None
    for d in range(2):
        for part in range(2):
            t = jnp.dot(s_ref[d, part].astype(BF16), w_ref[d, part], preferred_element_type=F32)
            ys = t if ys is None else ys + t
    half = ys.shape[1] // 2
    for g in range(2):
        u = u_ref[g]
        y = (jnp.dot(u.astype(BF16), mt_ref[g], preferred_element_type=F32) + u * d_ref[g]
             + ys[:, g * half:(g + 1) * half])
        y_ref[g] = jax.nn.gelu(y).astype(y_ref.dtype)


def _glu_epilogue(accs, x_ref, gate_ref):
    val, gate = accs
    return [x_ref[...] + gate_ref[...] * (val * jax.nn.sigmoid(gate))]


def _identity_prologue(z_ref):
    return z_ref[...]


def s5_mixer(geo, xc, modp, layer, gain, w_in, a_re, a_im, log_dt, b_re, b_im, c_re, c_im, d_skip, w_glu):
    B, L, Lc, D = geo.B, geo.L, geo.Lc, geo.D
    G, P = a_re.shape[1:]
    Cg, T = D // G, SSM_CHUNK
    TC = T * Cg
    nc_ctx, nc_lat = Lc // T, L // T
    rows = (nc_ctx + nc_lat) * B
    assert SUBLANES % B == 0 and (nc_ctx * B) % SUBLANES == 0 and rows % SUBLANES == 0 and G % 2 == 0
    (u,) = fused_matmul("s5_in", geo, geo.tiles(True), _norm_mod_inputs(geo, xc, gain, modp, layer, 0),
                        w_in.astype(BF16), 1, [], _norm_mod, lambda accs: accs, [F32])

    def to_chunks(v, n):
        return v.reshape(B, n // T, T, G, Cg).transpose(3, 1, 0, 2, 4).reshape(G, n // T, B, TC)

    U = jnp.concatenate([to_chunks(u[geo.RL:geo.Rv], Lc), to_chunks(u[:geo.RL], L)], axis=1)
    U = U.reshape(G, rows, TC)

    mt, bst, cst, aT = _s5_operators(a_re, a_im, log_dt, b_re, b_im, c_re, c_im, T)
    w_in_state = jnp.stack([jnp.stack([_pair_blockdiag(m) for m in bst[d]], axis=1) for d in range(2)],
                           axis=1).astype(BF16)
    w_out_state = jnp.stack([jnp.stack([_pair_blockdiag(m) for m in cst[d]], axis=1) for d in range(2)],
                            axis=1).astype(BF16)
    a_tab = jnp.stack([jnp.stack([v.reshape(1, G * P) for v in aT[d]]) for d in range(2)])
    d_lane = jnp.tile(d_skip.reshape(G, 1, Cg), (1, 1, T))

    P2 = 2 * P
    ds = pl.pallas_call(
        _s5_state_in_body,
        out_shape=jax.ShapeDtypeStruct((2, 2, rows, G * P), F32),
        grid=(G // 2,),
        in_specs=[pl.BlockSpec((2, rows, TC), lambda g: (g, 0, 0)),
                  pl.BlockSpec((None, 2, 2, 2 * TC, P2), lambda g: (g, 0, 0, 0, 0))],
        out_specs=pl.BlockSpec((2, 2, rows, P2), lambda g: (0, 0, 0, g)),
        compiler_params=_cparams("parallel"),
        name="s5_state_in",
    )(U, w_in_state)

    lb = _pick(G * P, (1024, 512, 256, 128))
    s_start = pl.pallas_call(
        functools.partial(_s5_scan_body, n_ctx_tiles=nc_ctx * B // SUBLANES,
                          n_tiles=rows // SUBLANES, B=B),
        out_shape=jax.ShapeDtypeStruct((2, 2, rows, G * P), F32),
        grid=(2, G * P // lb),
        in_specs=[pl.BlockSpec((None, 2, rows, lb), lambda d, j: (d, 0, 0, j)),
                  pl.BlockSpec((None, 2, 1, lb), lambda d, j: (d, 0, 0, j))],
        out_specs=pl.BlockSpec((None, 2, rows, lb), lambda d, j: (d, 0, 0, j)),
        compiler_params=_cparams("parallel", "parallel"),
        name="s5_scan",
    )(ds, a_tab)

    yg = pl.pallas_call(
        _s5_out_body,
        out_shape=jax.ShapeDtypeStruct((G, rows, TC), BF16),
        grid=(G // 2,),
        in_specs=[pl.BlockSpec((2, rows, TC), lambda g: (g, 0, 0)),
                  pl.BlockSpec((2, TC, TC), lambda g: (g, 0, 0)),
                  pl.BlockSpec((2, 1, TC), lambda g: (g, 0, 0)),
                  pl.BlockSpec((2, 2, rows, P2), lambda g: (0, 0, 0, g)),
                  pl.BlockSpec((None, 2, 2, P2, 2 * TC), lambda g: (g, 0, 0, 0, 0))],
        out_specs=pl.BlockSpec((2, rows, TC), lambda g: (g, 0, 0)),
        compiler_params=_cparams("parallel"),
        name="s5_out",
    )(U, mt.astype(BF16), d_lane, s_start, w_out_state)

    def from_chunks(v, n):
        return v.reshape(G, n // T, B, T, Cg).transpose(2, 1, 3, 0, 4).reshape(B * n, D)

    yg = yg.reshape(G, nc_ctx + nc_lat, B, TC)
    z = jnp.concatenate([from_chunks(yg[:, nc_ctx:], L), from_chunks(yg[:, :nc_ctx], Lc)], axis=0)
    lhs = [(z, _row_full(geo, D))]
    epi = [(xc, _row_tile(geo)), (modp, _mod_tile(geo, layer, 2))]
    (out,) = fused_matmul("s5_glu", geo, geo.tiles(True), lhs, w_glu.astype(BF16), 2, epi,
                          _identity_prologue, _glu_epilogue, [F32], alias={len(lhs) + 2: 0})
    return out


def _rope_tables(geo, head_dim):
    n_freq = head_dim // 4
    pos = jnp.arange(geo.L)
    inv_freq = ROPE_THETA ** (-jnp.arange(n_freq, dtype=F32) / n_freq)
    d = jnp.arange(LANES) % head_dim
    axis, half, f = d // (2 * n_freq), (d % (2 * n_freq)) // n_freq, d % n_freq
    p = jnp.where(axis[None, :] == 0, (pos // GRID_W)[:, None], (pos % GRID_W)[:, None]).astype(F32)
    ang = p * inv_freq[f][None, :]
    cos = jnp.cos(ang)
    sin = jnp.sin(ang) * jnp.where(half == 0, -1.0, 1.0)[None, :]
    rest = geo.R - geo.RL
    return (jnp.concatenate([jnp.tile(cos, (geo.B, 1)), jnp.ones((rest, LANES), F32)], axis=0),
            jnp.concatenate([jnp.tile(sin, (geo.B, 1)), jnp.zeros((rest, LANES), F32)], axis=0))


def _qkv_epilogue(accs, cos_ref, sin_ref, *, scale, n_freq):
    q, k, v = accs
    reps = q.shape[1] // LANES
    cos = jnp.concatenate([cos_ref[...]] * reps, axis=1)
    sin = jnp.concatenate([sin_ref[...]] * reps, axis=1)
    tn = q.shape[1]
    lane = lax.broadcasted_iota(jnp.int32, (1, tn), 1)
    first_half = (lane % (2 * n_freq)) < n_freq

    def rope(x):
        partner = jnp.where(first_half, pltpu.roll(x, tn - n_freq, axis=1), pltpu.roll(x, n_freq, axis=1))
        return x * cos + partner * sin

    return [rope(q) * scale, rope(k), v]


def _attn_body(q_ref, kl_ref, kc_ref, vl_ref, vc_ref, lam_ref, g_ref, o_ref, *, lambda_init):
    q = q_ref[...]
    head_dim = q.shape[1] // 2
    lane = lax.broadcasted_iota(jnp.int32, (1, q.shape[1]), 1)
    nt = (((1,), (1,)), ((), ()))

    def attend(qm):
        s_l = lax.dot_general(qm, kl_ref[...], nt, preferred_element_type=F32)
        s_c = lax.dot_general(qm, kc_ref[...], nt, preferred_element_type=F32)
        m = jnp.maximum(jnp.max(s_l, axis=1, keepdims=True), jnp.max(s_c, axis=1, keepdims=True))
        p_l = jnp.exp(s_l - m)
        p_c = jnp.exp(s_c - m)
        den = jnp.sum(p_l, axis=1, keepdims=True) + jnp.sum(p_c, axis=1, keepdims=True)
        o = (jnp.dot(p_l.astype(BF16), vl_ref[...], preferred_element_type=F32)
             + jnp.dot(p_c.astype(BF16), vc_ref[...], preferred_element_type=F32))
        return o / den

    lp = lam_ref[...]
    lam = (jnp.exp(jnp.sum(lp[0:1] * lp[1:2], axis=1, keepdims=True))
           - jnp.exp(jnp.sum(lp[2:3] * lp[3:4], axis=1, keepdims=True)) + lambda_init)
    o = (attend(jnp.where(lane < head_dim, q, jnp.zeros_like(q)))
         - lam * attend(jnp.where(lane >= head_dim, q, jnp.zeros_like(q))))
    o = o * lax.rsqrt(jnp.mean(o * o, axis=-1, keepdims=True) + EPS) * g_ref[...]
    o_ref[...] = (o * (1.0 - lambda_init)).astype(o_ref.dtype)


def attn_mixer(geo, xc, modp, layer, gain, w_qkv, lam_params, subln_g, w_o):
    B, L, Lc, D = geo.B, geo.L, geo.Lc, geo.D
    head_dim = lam_params.shape[1]
    hw = 2 * head_dim
    assert hw == LANES and geo.R % Lc == 0
    H = D // hw
    lambda_init = 0.8 - 0.6 * math.exp(-0.3 * layer)
    cos, sin = _rope_tables(geo, head_dim)
    tab = pl.BlockSpec((geo.tm, LANES), lambda i, j: (i, 0))
    q, k, v = fused_matmul(
        "attn_qkv", geo, geo.tiles(True), _norm_mod_inputs(geo, xc, gain, modp, layer, 0),
        w_qkv.astype(BF16), 3, [(cos, tab), (sin, tab)], _norm_mod,
        functools.partial(_qkv_epilogue, scale=head_dim ** -0.5, n_freq=head_dim // 4), [BF16] * 3)
    tq = _pick(L, (256, 128, 64, 32, 16))
    nq = L // tq
    ctx_blk = geo.RL // Lc
    lat_kv = pl.BlockSpec((L, hw), lambda b, h, i: (b, h))
    ctx_kv = pl.BlockSpec((Lc, hw), lambda b, h, i: (ctx_blk + b, h))
    o = pl.pallas_call(
        functools.partial(_attn_body, lambda_init=lambda_init),
        out_shape=jax.ShapeDtypeStruct((geo.R, D), BF16),
        grid=(B, H, nq),
        in_specs=[pl.BlockSpec((tq, hw), lambda b, h, i: (b * nq + i, h)),
                  lat_kv, ctx_kv, lat_kv, ctx_kv,
                  pl.BlockSpec((4, head_dim), lambda b, h, i: (0, 0)),
                  pl.BlockSpec((1, hw), lambda b, h, i: (0, 0))],
        out_specs=pl.BlockSpec((tq, hw), lambda b, h, i: (b * nq + i, h)),
        compiler_params=_cparams("parallel", "parallel", "arbitrary"),
        name="diff_attention",
    )(q, k, k, v, v, lam_params, subln_g.reshape(1, hw))
    lhs = [(o, _row_full(geo, D))]
    epi = [(xc, _row_tile(geo)), (modp, _mod_tile(geo, layer, 2))]
    (out,) = fused_matmul("attn_out", geo, geo.tiles(False), lhs, w_o.astype(BF16), 1, epi,
                          _identity_prologue, _residual_epilogue, [F32], alias={len(lhs) + 1: 0})
    return out


def _router_body(x_ref, g_ref, sh_ref, sc_ref, rt_ref, hf_ref, aff_ref):
    h = _norm_mod(x_ref, g_ref, sh_ref, sc_ref)
    half = h.shape[1] // 2
    hb = h.astype(BF16)
    hbf = hb.astype(F32)
    hi = pltpu.bitcast(hbf[:, :half], jnp.uint32)
    lo = lax.shift_right_logical(pltpu.bitcast(hbf[:, half:], jnp.uint32), jnp.uint32(16))
    hf_ref[...] = hi | lo
    h_lo = (h - hbf).astype(BF16)
    rt = rt_ref[...]
    r_hi = rt.astype(BF16)
    r_lo = (rt - r_hi.astype(F32)).astype(BF16)
    nt = (((1,), (1,)), ((), ()))
    logits = (lax.dot_general(r_hi, hb, nt, preferred_element_type=F32)
              + lax.dot_general(r_hi, h_lo, nt, preferred_element_type=F32)
              + lax.dot_general(r_lo, hb, nt, preferred_element_type=F32))
    m = jnp.max(logits, axis=0, keepdims=True)
    e = jnp.exp(logits - m)
    aff_ref[...] = e / jnp.sum(e, axis=0, keepdims=True)


def moe_router(geo, xc, modp, layer, gain, router, with_ctx):
    D, tm = geo.D, geo.tm
    E = router.shape[1]
    i_only = lambda f: (lambda i: f(i, 0))
    specs = [s for _, s in _norm_mod_inputs(geo, xc, gain, modp, layer, 3)]
    specs = [pl.BlockSpec(s.block_shape, i_only(s.index_map)) for s in specs]
    return pl.pallas_call(
        _router_body,
        out_shape=[jax.ShapeDtypeStruct((geo.R, D // 2), jnp.uint32),
                   jax.ShapeDtypeStruct((E, geo.R), F32)],
        grid=(geo.tiles(with_ctx),),
        in_specs=specs + [pl.BlockSpec((E, D), lambda i: (0, 0))],
        out_specs=[pl.BlockSpec((tm, D // 2), lambda i: (i, 0)),
                   pl.BlockSpec((E, tm), lambda i: (0, i))],
        compiler_params=_cparams("parallel"),
        name="moe_router",
    )(xc, gain.reshape(1, D), modp, modp, router.T)


def _prefix_count(x, tri):
    n = x.shape[1]
    off = jnp.zeros((x.shape[0], 1), F32)
    parts = []
    for j in range(n // LANES):
        blk = x[:, j * LANES:(j + 1) * LANES]
        inc = jnp.dot(blk.astype(BF16), tri, preferred_element_type=F32)
        parts.append(inc - blk + off)
        off = off + inc[:, LANES - 1:LANES]
    return jnp.concatenate(parts, axis=1)


def _select_body(aff_ref, idx_ref, g_ref, *, cap):
    a = aff_ref[...]
    E, n = a.shape
    bits = pltpu.bitcast(a, jnp.int32)
    thr = jnp.zeros((E, 1), jnp.int32)
    for k in range(30, -1, -1):
        cand = thr | jnp.int32(1 << k)
        cnt = jnp.sum((bits >= cand).astype(F32), axis=1, keepdims=True)
        thr = jnp.where(cnt >= cap, cand, thr)
    gt = bits > thr
    eq = bits == thr
    need = cap - jnp.sum(gt.astype(F32), axis=1, keepdims=True)
    r = lax.broadcasted_iota(jnp.int32, (LANES, LANES), 0)
    c = lax.broadcasted_iota(jnp.int32, (LANES, LANES), 1)
    tri = (r <= c).astype(BF16)
    sel = gt | (eq & (_prefix_count(eq.astype(F32), tri) < need))
    self32 = sel.astype(F32)
    slot = _prefix_count(self32, tri)
    slot = jnp.where(sel, slot, -1.0)
    t = lax.broadcasted_iota(jnp.int32, (1, n), 1)
    t_hi = (t // 64).astype(F32)
    t_lo = (t % 64).astype(F32)
    a1 = a.astype(BF16).astype(F32)
    a2 = (a - a1).astype(BF16).astype(F32)
    a3 = (a - a1 - a2).astype(BF16).astype(F32)
    s_iota = lax.broadcasted_iota(jnp.int32, (cap, 1), 0).astype(F32)
    row = lax.broadcasted_iota(jnp.int32, (SUBLANES, 1), 0)
    nt = (((1,), (1,)), ((), ()))
    for e in range(E):
        onehot = jnp.where(slot[e:e + 1, :] == s_iota, 1.0, 0.0).astype(BF16)
        lhs = jnp.where(row == 0, t_hi, jnp.where(row == 1, t_lo, jnp.where(
            row == 2, a1[e:e + 1, :], jnp.where(row == 3, a2[e:e + 1, :], jnp.where(
                row == 4, a3[e:e + 1, :], 0.0))))).astype(BF16)
        res = lax.dot_general(lhs, onehot, nt, preferred_element_type=F32)
        idx_ref[e:e + 1, :] = (res[0:1, :] * 64.0 + res[1:2, :]).astype(jnp.int32)
        g_ref[e:e + 1, :] = res[2:3, :] + res[3:4, :] + res[4:5, :]


def moe_select(aff, n_samples, n, col_off):
    E = aff.shape[0]
    cap = max(1, CAPACITY_FACTOR * n // E)
    blk0 = col_off // n
    return pl.pallas_call(
        functools.partial(_select_body, cap=cap),
        out_shape=[jax.ShapeDtypeStruct((n_samples, E, cap), jnp.int32),
                   jax.ShapeDtypeStruct((n_samples, E, cap), F32)],
        grid=(n_samples,),
        in_specs=[pl.BlockSpec((E, n), lambda s: (0, blk0 + s))],
        out_specs=[pl.BlockSpec((None, E, cap), lambda s: (s, 0, 0)),
                   pl.BlockSpec((None, E, cap), lambda s: (s, 0, 0))],
        compiler_params=_cparams("parallel"),
        name="moe_select",
    )(aff)


def _gather_body(idx_ref, hf_ref, prev_ref, o_ref, *, cap, n_exp):
    del prev_ref
    base = (pl.program_id(0) * n_exp + pl.program_id(1)) * cap

    def step(i, carry):
        i0 = pl.multiple_of(i * SUBLANES, SUBLANES)
        rows = [hf_ref[pl.ds(idx_ref[base + i0 + k], 1), :] for k in range(SUBLANES)]
        for k in range(SUBLANES):
            o_ref[pl.ds(i0 + k, 1), :] = rows[k]
        return carry

    lax.fori_loop(0, cap // SUBLANES, step, 0)


def moe_gather(idx, hf, xg, n, row_off, slot_off):
    S, E, cap = idx.shape
    W = hf.shape[1]
    rb, sb = row_off // n, slot_off // cap
    grid_spec = pltpu.PrefetchScalarGridSpec(
        num_scalar_prefetch=1,
        grid=(S, E),
        in_specs=[pl.BlockSpec((n, W), lambda s, e, idx: (rb + s, 0)),
                  pl.BlockSpec(memory_space=pl.ANY)],
        out_specs=pl.BlockSpec((None, cap, W), lambda s, e, idx: (e, sb + s, 0)),
    )
    return pl.pallas_call(
        functools.partial(_gather_body, cap=cap, n_exp=E),
        out_shape=jax.ShapeDtypeStruct(xg.shape, xg.dtype),
        grid_spec=grid_spec,
        input_output_aliases={2: 0},
        compiler_params=_cparams("parallel", "arbitrary"),
        name="moe_gather",
    )(idx.reshape(-1), hf, xg)


def _expert_body(xg_ref, wg_ref, wu_ref, wd_ref, y_ref):
    u = xg_ref[...]
    half = u.shape[1]
    a = pltpu.bitcast(u & jnp.uint32(0xFFFF0000), F32).astype(BF16)
    b = pltpu.bitcast(lax.shift_left(u, jnp.uint32(16)), F32).astype(BF16)
    gate = (jnp.dot(a, wg_ref[:half, :], preferred_element_type=F32)
            + jnp.dot(b, wg_ref[half:, :], preferred_element_type=F32))
    up = (jnp.dot(a, wu_ref[:half, :], preferred_element_type=F32)
          + jnp.dot(b, wu_ref[half:, :], preferred_element_type=F32))
    act = (gate * jax.nn.sigmoid(gate) * up).astype(BF16)
    y_ref[...] = jnp.dot(act, wd_ref[...], preferred_element_type=F32)


def moe_experts(xg, rows, w_gate, w_up, w_down):
    E, rows_alloc, W = xg.shape
    D, F = w_gate.shape[1], w_gate.shape[2]
    tr = next(t for t in (512, 256, 128, 64, 32, 16)
              if rows_alloc % t == 0 and (-rows % t) * 8 <= rows)
    return pl.pallas_call(
        _expert_body,
        out_shape=jax.ShapeDtypeStruct((E, rows_alloc, D), F32),
        grid=(E, -(-rows // tr)),
        in_specs=[pl.BlockSpec((None, tr, W), lambda e, r: (e, r, 0)),
                  pl.BlockSpec((None, D, F), lambda e, r: (e, 0, 0)),
                  pl.BlockSpec((None, D, F), lambda e, r: (e, 0, 0)),
                  pl.BlockSpec((None, F, D), lambda e, r: (e, 0, 0))],
        out_specs=pl.BlockSpec((None, tr, D), lambda e, r: (e, r, 0)),
        compiler_params=_cparams("parallel", "arbitrary"),
        name="moe_experts",
    )(xg, w_gate, w_up, w_down)


def _combine_body(idx_ref, g_ref, y_ref, x_ref, gate_ref, o_ref, *, cap, n_exp):
    e = pl.program_id(2)
    base = (pl.program_id(0) * n_exp + e) * cap

    @pl.when(e == 0)
    def _():
        o_ref[...] = jnp.zeros_like(o_ref)

    def step(i, carry):
        i0 = pl.multiple_of(i * SUBLANES, SUBLANES)
        y8 = y_ref[pl.ds(i0, SUBLANES), :]
        tok = [idx_ref[base + i0 + k] for k in range(SUBLANES)]
        acc = [o_ref[pl.ds(tok[k], 1), :] for k in range(SUBLANES)]
        for k in range(SUBLANES):
            o_ref[pl.ds(tok[k], 1), :] = acc[k] + g_ref[base + i0 + k] * y8[k:k + 1, :]
        return carry

    lax.fori_loop(0, cap // SUBLANES, step, 0)

    @pl.when(e == n_exp - 1)
    def _():
        o_ref[...] = x_ref[...] + gate_ref[...] * o_ref[...]


def moe_combine(geo, idx, g, y, xc, modp, layer, n, row_off, slot_off, mod_ctx):
    S, E, cap = idx.shape
    D = geo.D
    dc = _pick(D, (512, 256, 128))
    rb, sb = row_off // n, slot_off // cap
    mod_row = (lambda s: geo.B) if mod_ctx else (lambda s: s)
    grid_spec = pltpu.PrefetchScalarGridSpec(
        num_scalar_prefetch=1,
        grid=(S, D // dc, E),
        in_specs=[pl.BlockSpec(memory_space=pltpu.SMEM),
                  pl.BlockSpec((None, cap, dc), lambda s, c, e, idx: (e, sb + s, c)),
                  pl.BlockSpec((n, dc), lambda s, c, e, idx: (rb + s, c)),
                  pl.BlockSpec((None, None, None, 1, dc),
                               lambda s, c, e, idx: (layer, mod_row(s), 5, 0, c))],
        out_specs=pl.BlockSpec((n, dc), lambda s, c, e, idx: (rb + s, c)),
    )
    return pl.pallas_call(
        functools.partial(_combine_body, cap=cap, n_exp=E),
        out_shape=jax.ShapeDtypeStruct(xc.shape, xc.dtype),
        grid_spec=grid_spec,
        input_output_aliases={3: 0},
        compiler_params=_cparams("parallel", "parallel", "arbitrary"),
        name="moe_combine",
    )(idx.reshape(-1), g.reshape(-1), y, xc, modp)


def moe_layer(geo, xc, modp, layer, gain, router, w_gate, w_up, w_down, with_ctx):
    B, L, Lc = geo.B, geo.L, geo.Lc
    E = router.shape[1]
    hf, aff = moe_router(geo, xc, modp, layer, gain, router, with_ctx)
    groups = [(L, 0)] + ([(Lc, geo.RL)] if with_ctx else [])
    sel = [moe_select(aff, B, n, off) for n, off in groups]
    slot_offs = [0, B * sel[0][0].shape[2]]
    rows = sum(B * s[0].shape[2] for s in sel)
    cap_l = sel[0][0].shape[2]
    xg = jnp.zeros((E, -(-rows // cap_l) * cap_l, geo.D // 2), jnp.uint32)
    for (n, off), (idx, _), so in zip(groups, sel, slot_offs):
        xg = moe_gather(idx, hf, xg, n, off, so)
    y = moe_experts(xg, rows, w_gate.astype(BF16), w_up.astype(BF16), w_down.astype(BF16))
    for k, ((n, off), (idx, g), so) in enumerate(zip(groups, sel, slot_offs)):
        xc = moe_combine(geo, idx, g, y, xc, modp, layer, n, off, so, mod_ctx=k == 1)
    return xc


def _final_norm_body(x_ref, g_ref, o_ref):
    x = x_ref[...]
    o_ref[...] = x * lax.rsqrt(jnp.mean(x * x, axis=-1, keepdims=True) + EPS) * g_ref[...]


def final_norm(geo, xc, gain):
    D, tm = geo.D, geo.tm
    return pl.pallas_call(
        _final_norm_body,
        out_shape=jax.ShapeDtypeStruct((geo.RL, D), F32),
        grid=(geo.RL // tm,),
        in_specs=[pl.BlockSpec((tm, D), lambda i: (i, 0)), pl.BlockSpec((1, D), lambda i: (0, 0))],
        out_specs=pl.BlockSpec((tm, D), lambda i: (i, 0)),
        compiler_params=_cparams("parallel"),
        name="final_norm",
    )(xc, gain.reshape(1, D))


def kernel(x, c, ctx, c_ctx, ada_w, ada_b, norm_mix_g, norm_ffn_g, final_norm_g, conv_w_in, conv_w, conv_w_out, ssm_w_in, ssm_a_re, ssm_a_im, ssm_log_dt, ssm_b_re, ssm_b_im, ssm_c_re, ssm_c_im, ssm_d, ssm_w_glu, attn_w_qkv, attn_lambda, attn_subln_g, attn_w_o, moe_router, moe_w_gate, moe_w_up, moe_w_down):
    B, L, D = x.shape
    Lc = ctx.shape[1]
    depth = ada_w.shape[0]
    geo = Geo(B, L, Lc, D)
    xc = jnp.concatenate([x.reshape(B * L, D), ctx.reshape(B * Lc, D),
                          jnp.zeros((geo.R - geo.Rv, D), F32)], axis=0)
    c_all = jnp.zeros((SUBLANES, D), F32).at[:B].set(c).at[B].set(c_ctx)
    modp = ada_modulation(c_all, ada_w, ada_b).reshape(depth, SUBLANES, 6, 1, D)
    ctx_live = [any(MIXER_READS_CTX[j % N_MIXERS] for j in range(i + 1, depth)) for i in range(depth)]
    for i in range(depth):
        kind, slot = i % N_MIXERS, i // N_MIXERS
        ctx_out = ctx_live[i]
        if kind == 0:
            xc = conv_mixer(geo, xc, modp, i, norm_mix_g[i], conv_w_in[slot], conv_w[slot],
                            conv_w_out[slot], ctx_out)
        elif kind == 1:
            xc = s5_mixer(geo, xc, modp, i, norm_mix_g[i], ssm_w_in[slot], ssm_a_re[slot], ssm_a_im[slot],
                          ssm_log_dt[slot], ssm_b_re[slot], ssm_b_im[slot], ssm_c_re[slot], ssm_c_im[slot],
                          ssm_d[slot], ssm_w_glu[slot])
        else:
            xc = attn_mixer(geo, xc, modp, i, norm_mix_g[i], attn_w_qkv[slot], attn_lambda[slot],
                            attn_subln_g[slot], attn_w_o[slot])
        xc = moe_layer(geo, xc, modp, i, norm_ffn_g[i], moe_router[i], moe_w_gate[i],
                       moe_w_up[i], moe_w_down[i], ctx_out)
    return final_norm(geo, xc, final_norm_g).reshape(B, L, D)
```

```python
import functools
import math

import jax
import jax.numpy as jnp
from jax import lax
from jax.experimental import pallas as pl
from jax.experimental.pallas import tpu as pltpu

F32 = jnp.float32
BF16 = jnp.bfloat16
EPS = 1e-6
N_MIXERS = 3
MIXER_READS_CTX = (False, True, True)
GRID_W = 64
ROPE_THETA = 10000.0
CAPACITY_FACTOR = 2
SSM_CHUNK = 16
SUBLANES = 8
CONV_HALO = 16
ATTN_SUBTILES = 4
LANES = 128
VMEM_LIMIT = 56 * 1024 * 1024


def _cparams(*sem):
    return pltpu.CompilerParams(dimension_semantics=sem, vmem_limit_bytes=VMEM_LIMIT)


def _pick(n, cands):
    for c in cands:
        if n % c == 0:
            return c
    raise ValueError(f"no tile in {cands} divides {n}")


class Geo:
    def __init__(self, B, L, Lc, D, tm=None):
        self.B, self.L, self.Lc, self.D = B, L, Lc, D
        self.RL = B * L
        self.Rv = B * (L + Lc)
        self.R = -(-self.Rv // L) * L
        self.tm = tm or _pick(math.gcd(L, B * Lc), (512, 256, 128, 64, 32, 16, 8))
        assert L % self.tm == 0 and (B * Lc) % self.tm == 0
        self.tn = _pick(D, (1024, 512, 256, 128))

    def mod_row(self, i):
        return jnp.minimum((i * self.tm) // self.L, self.B)

    def tiles(self, with_ctx):
        return (self.Rv if with_ctx else self.RL) // self.tm


def _ada_body(c_ref, w_ref, b_ref, o_ref):
    c = c_ref[...]
    s = (c * jax.nn.sigmoid(c)).astype(BF16)
    o_ref[...] = jnp.dot(s, w_ref[...].astype(BF16), preferred_element_type=F32) + b_ref[...]


def ada_modulation(c_all, ada_w, ada_b):
    depth, D, N = ada_w.shape
    tn = _pick(N, (1024, 512, 256, 128))
    return pl.pallas_call(
        _ada_body,
        out_shape=jax.ShapeDtypeStruct((depth, SUBLANES, N), F32),
        grid=(depth, N // tn),
        in_specs=[pl.BlockSpec((SUBLANES, D), lambda l, j: (0, 0)),
                  pl.BlockSpec((None, D, tn), lambda l, j: (l, 0, j)),
                  pl.BlockSpec((None, 1, tn), lambda l, j: (l, 0, j))],
        out_specs=pl.BlockSpec((None, SUBLANES, tn), lambda l, j: (l, 0, j)),
        compiler_params=_cparams("parallel", "parallel"),
        name="ada_modulation",
    )(c_all, ada_w, ada_b.reshape(depth, 1, N))


def _mm_body(*refs, n_lhs, n_w, n_epi, prologue, epilogue):
    lhs = refs[:n_lhs]
    ws = refs[n_lhs:n_lhs + n_w]
    epi = refs[n_lhs + n_w:n_lhs + n_w + n_epi]
    outs = refs[n_lhs + n_w + n_epi:-1]
    hb = refs[-1]

    @pl.when(pl.program_id(1) == 0)
    def _():
        hb[...] = prologue(*lhs).astype(hb.dtype)

    h = hb[...]
    accs = [jnp.dot(h, w[...], preferred_element_type=F32) for w in ws]
    for o, v in zip(outs, epilogue(accs, *epi)):
        o[...] = v.astype(o.dtype).reshape(o.shape)


def fused_matmul(name, geo, n_tiles, lhs, w, n_split, epi, prologue, epilogue, outs, alias=None,
                 out_layout=None):
    K = w.shape[0]
    N = w.shape[1] // n_split
    tm, tn = geo.tm, geo.tn
    nj = N // tn
    w_specs = [pl.BlockSpec((K, tn), functools.partial(lambda i, j, s: (0, s * nj + j), s=s))
               for s in range(n_split)]
    arrays = [a for a, _ in lhs] + [w] * n_split + [a for a, _ in epi]
    specs = [s for _, s in lhs] + w_specs + [s for _, s in epi]
    body = functools.partial(_mm_body, n_lhs=len(lhs), n_w=n_split, n_epi=len(epi),
                             prologue=prologue, epilogue=epilogue)
    out_shape, out_spec = out_layout or ((geo.R, N), pl.BlockSpec((tm, tn), lambda i, j: (i, j)))
    res = pl.pallas_call(
        body,
        out_shape=[jax.ShapeDtypeStruct(out_shape, dt) for dt in outs],
        grid=(n_tiles, nj),
        in_specs=specs,
        out_specs=[out_spec for _ in outs],
        scratch_shapes=[pltpu.VMEM((tm, K), BF16)],
        input_output_aliases=alias or {},
        compiler_params=_cparams("parallel", "arbitrary"),
        name=name,
    )(*arrays)
    return res


def _row_full(geo, K):
    return pl.BlockSpec((geo.tm, K), lambda i, j: (i, 0))


def _row_tile(geo):
    return pl.BlockSpec((geo.tm, geo.tn), lambda i, j: (i, j))


def _vec_full(K):
    return pl.BlockSpec((1, K), lambda i, j: (0, 0))


def _mod_full(geo, layer, k):
    return pl.BlockSpec((None, None, None, 1, geo.D),
                        lambda i, j: (layer, geo.mod_row(i), k, 0, 0))


def _mod_tile(geo, layer, k):
    return pl.BlockSpec((None, None, None, 1, geo.tn),
                        lambda i, j: (layer, geo.mod_row(i), k, 0, j))


def _norm_mod(x_ref, g_ref, sh_ref, sc_ref):
    x = x_ref[...]
    y = x * lax.rsqrt(jnp.mean(x * x, axis=-1, keepdims=True) + EPS)
    return (y * g_ref[...]) * (1.0 + sc_ref[...]) + sh_ref[...]


def _norm_mod_inputs(geo, xc, gain, modp, layer, k_shift):
    return [(xc, _row_full(geo, geo.D)), (gain.reshape(1, geo.D), _vec_full(geo.D)),
            (modp, _mod_full(geo, layer, k_shift)), (modp, _mod_full(geo, layer, k_shift + 1))]


def _residual_epilogue(accs, x_ref, gate_ref):
    return [x_ref[...] + gate_ref[...] * accs[0]]


def _conv_in_epilogue(accs):
    gate_b, gate_c, v = accs
    return [gate_b, gate_c * v]


def _conv_prologue(gb_ref, z_ref, zp_ref, zn_ref, wc_ref, *, geo):
    tm = geo.tm
    z = z_ref[...].astype(F32)
    t = lax.broadcasted_iota(jnp.int32, (tm, 1), 0)
    row = pl.program_id(0) * tm + t
    is_lat = row < geo.RL
    pos = jnp.where(is_lat, row % geo.L, (row - geo.RL) % geo.Lc)
    first = pos == 0
    last = pos == jnp.where(is_lat, geo.L - 1, geo.Lc - 1)
    halo_last = zp_ref[CONV_HALO - 1:CONV_HALO, :].astype(F32)
    z_prev = jnp.where(t == 0, halo_last, pltpu.roll(z, 1, axis=0))
    z_prev = jnp.where(first, 0.0, z_prev)
    z_next = jnp.where(t == tm - 1, zn_ref[0:1, :].astype(F32), pltpu.roll(z, tm - 1, axis=0))
    z_next = jnp.where(last, 0.0, z_next)
    wc = wc_ref[...]
    conv = wc[0:1, :] * z_prev + wc[1:2, :] * z + wc[2:3, :] * z_next
    return gb_ref[...].astype(F32) * conv


def conv_mixer(geo, xc, modp, layer, gain, w_in, w_conv, w_out, with_ctx):
    D, tm = geo.D, geo.tm
    n_tiles = geo.tiles(with_ctx)
    gb, z = fused_matmul("conv_in", geo, n_tiles,
                         _norm_mod_inputs(geo, xc, gain, modp, layer, 0),
                         w_in.astype(BF16), 3, [], _norm_mod, _conv_in_epilogue, [BF16, BF16])
    r8 = tm // CONV_HALO
    last_blk = geo.R // CONV_HALO - 1
    halo_prev = pl.BlockSpec((CONV_HALO, D), lambda i, j: (jnp.maximum(i * r8 - 1, 0), 0))
    halo_next = pl.BlockSpec((CONV_HALO, D), lambda i, j: (jnp.minimum((i + 1) * r8, last_blk), 0))
    lhs = [(gb, _row_full(geo, D)), (z, _row_full(geo, D)), (z, halo_prev), (z, halo_next),
           (w_conv, pl.BlockSpec((3, D), lambda i, j: (0, 0)))]
    epi = [(xc, _row_tile(geo)), (modp, _mod_tile(geo, layer, 2))]
    (out,) = fused_matmul("conv_out", geo, n_tiles, lhs, w_out.astype(BF16), 1, epi,
                          functools.partial(_conv_prologue, geo=geo), _residual_epilogue, [F32],
                          alias={len(lhs) + 1: 0})
    return out


def _s5_operators(a_re, a_im, log_dt, b_re, b_im, c_re, c_im, T):
    G, P = a_re.shape[1:]
    Cg = b_re.shape[-1]
    k = jnp.arange(T + 1, dtype=F32)[:, None, None]
    mt = 0.0
    bst, cst, aT = [], [], []
    lag = jnp.arange(T)[:, None] - jnp.arange(T)[None, :]
    for d in range(2):
        dt = jnp.exp(log_dt[d])[:, None]
        mag = jnp.exp(k * dt * a_re[d])
        pw_r, pw_i = mag * jnp.cos(k * dt * a_im[d]), mag * jnp.sin(k * dt * a_im[d])
        den = a_re[d] * a_re[d] + a_im[d] * a_im[d]
        nr = (pw_r[1] - 1.0) * a_re[d] + pw_i[1] * a_im[d]
        ni = pw_i[1] * a_re[d] - (pw_r[1] - 1.0) * a_im[d]
        fr, fi = (nr / den)[..., None], (ni / den)[..., None]
        bb_r = fr * b_re[d] - fi * b_im[d]
        bb_i = fr * b_im[d] + fi * b_re[d]
        w_r = c_re[d][None] * pw_r[:T, :, None, :] - c_im[d][None] * pw_i[:T, :, None, :]
        w_i = c_re[d][None] * pw_i[:T, :, None, :] + c_im[d][None] * pw_r[:T, :, None, :]
        kk = (jnp.einsum('kgjp,gpc->gkjc', w_r, bb_r, precision='highest')
              - jnp.einsum('kgjp,gpc->gkjc', w_i, bb_i, precision='highest'))
        dlag = lag if d == 0 else -lag
        m = jnp.where((dlag >= 0)[None, :, :, None, None], kk[:, jnp.clip(dlag, 0, T - 1)], 0.0)
        mt = mt + m.transpose(0, 2, 4, 1, 3).reshape(G, T * Cg, T * Cg)
        e = jnp.arange(T - 1, -1, -1) if d == 0 else jnp.arange(T)
        er, ei = pw_r[e], pw_i[e]
        s_r = er[:, :, :, None] * bb_r[None] - ei[:, :, :, None] * bb_i[None]
        s_i = er[:, :, :, None] * bb_i[None] + ei[:, :, :, None] * bb_r[None]
        bst.append(tuple(v.transpose(1, 0, 3, 2).reshape(G, T * Cg, P) for v in (s_r, s_i)))
        e = jnp.arange(1, T + 1) if d == 0 else jnp.arange(T, 0, -1)
        er, ei = pw_r[e], pw_i[e]
        o_r = c_re[d][None] * er[:, :, None, :] - c_im[d][None] * ei[:, :, None, :]
        o_i = -(c_re[d][None] * ei[:, :, None, :] + c_im[d][None] * er[:, :, None, :])
        cst.append(tuple(v.transpose(1, 3, 0, 2).reshape(G, P, T * Cg) for v in (o_r, o_i)))
        aT.append((pw_r[T], pw_i[T]))
    return mt, bst, cst, aT


def _lane_tile_operators(mt, bst, cst, T, gl):
    G, TC, _ = mt.shape
    Cg = TC // T
    P = bst[0][0].shape[-1]
    GB = G // gl
    eye = jnp.eye(gl, dtype=F32)
    w_intra = jnp.einsum('bgxcyj,gh->bxgcyhj', mt.reshape(GB, gl, T, Cg, T, Cg), eye)
    w_intra = w_intra.reshape(GB, T * gl * Cg, T * gl * Cg)
    w_in = [jnp.einsum('bgxcp,gh->bxgchp', m.reshape(GB, gl, T, Cg, P), eye).reshape(GB, T * gl * Cg, gl * P)
            for d in range(2) for m in bst[d]]
    w_read = [jnp.einsum('bgpyj,gh->bgpyhj', m.reshape(GB, gl, P, T, Cg), eye).reshape(GB, gl * P, T * gl * Cg)
              for d in range(2) for m in cst[d]]
    return (jnp.concatenate(w_in, axis=2).astype(BF16),
            jnp.concatenate([w_intra] + w_read, axis=1).astype(BF16))


def _s5_state_in_body(*refs, T):
    u_refs, w_ref, ds_ref = refs[:T], refs[T], refs[T + 1]
    u = jnp.concatenate([r[...] for r in u_refs], axis=1)
    res = jnp.dot(u, w_ref[...], preferred_element_type=F32)
    w = res.shape[1] // 4
    for k in range(4):
        ds_ref[k // 2, k % 2] = res[:, k * w:(k + 1) * w]


def _s5_scan_body(ds_ref, a_ref, s_ref, *, B, nc_lat, nc_ctx):
    a_re, a_im = a_ref[0], a_ref[1]
    zero = jnp.zeros((B, ds_ref.shape[-1]), F32)

    def segment(first, n, reverse, state):
        def step(i, st):
            s_re, s_im = st
            k = first + ((n - 1 - i) if reverse else i)
            for b in range(B):
                s_ref[0, pl.ds(k + b * n, 1), :] = s_re[b:b + 1].astype(s_ref.dtype)
                s_ref[1, pl.ds(k + b * n, 1), :] = s_im[b:b + 1].astype(s_ref.dtype)
            d_re = jnp.concatenate([ds_ref[0, pl.ds(k + b * n, 1), :] for b in range(B)], axis=0)
            d_im = jnp.concatenate([ds_ref[1, pl.ds(k + b * n, 1), :] for b in range(B)], axis=0)
            return a_re * s_re - a_im * s_im + d_re, a_re * s_im + a_im * s_re + d_im
        return lax.fori_loop(0, n, step, state)

    for d in range(2):
        @pl.when(pl.program_id(0) == d)
        def _():
            st = segment(B * nc_lat, nc_ctx, d == 1, (zero, zero))
            segment(0, nc_lat, d == 1, st)


def _s5_out_body(*refs, T):
    u_refs, s_ref, w_ref, d_ref, y_ref = refs[:T], refs[T], refs[T + 1], refs[T + 2], refs[T + 3]
    u = jnp.concatenate([r[...] for r in u_refs], axis=1)
    s = jnp.concatenate([s_ref[d, part] for d in range(2) for part in range(2)], axis=1).astype(u.dtype)
    y = (jnp.dot(jnp.concatenate([u, s], axis=1), w_ref[...], preferred_element_type=F32)
         + u.astype(F32) * d_ref[...])
    z = jax.nn.gelu(y).astype(y_ref.dtype)
    for t in range(T):
        y_ref[t] = z[:, t * LANES:(t + 1) * LANES]


def _glu_epilogue(accs, x_ref, gate_ref):
    val, gate = accs
    return [x_ref[...] + gate_ref[...] * (val * jax.nn.sigmoid(gate))]


def _identity_prologue(z_ref):
    return z_ref[...]


def _untranspose_prologue(z_ref):
    T, cb, D = z_ref.shape
    v = z_ref[...].reshape(T * cb, D)
    r = lax.broadcasted_iota(jnp.int32, (T * cb, T * cb), 0)
    c = lax.broadcasted_iota(jnp.int32, (T * cb, T * cb), 1)
    perm = jnp.where(c == (r % T) * cb + r // T, 1.0, 0.0).astype(v.dtype)
    return jnp.dot(perm, v, preferred_element_type=F32)


def s5_mixer(geo, xc, modp, layer, gain, w_in, a_re, a_im, log_dt, b_re, b_im, c_re, c_im, d_skip, w_glu):
    B, L, Lc, D = geo.B, geo.L, geo.Lc, geo.D
    G, P = a_re.shape[1:]
    Cg, T = D // G, SSM_CHUNK
    gl = LANES // Cg
    GB = G // gl
    nc_ctx, nc_lat = Lc // T, L // T
    rows = geo.Rv // T
    rt = _pick(rows, (272, 256, 128, 64, 32, 16))
    (u,) = fused_matmul("s5_in", geo, geo.tiles(True), _norm_mod_inputs(geo, xc, gain, modp, layer, 0),
                        w_in.astype(BF16), 1, [], _norm_mod, lambda accs: accs, [BF16])
    u2 = u.reshape(geo.R // T, T * D)
    u_specs = [pl.BlockSpec((rt, LANES), functools.partial(lambda gb, r, t: (r, t * (D // LANES) + gb), t=t))
               for t in range(T)]

    mt, bst, cst, aT = _s5_operators(a_re, a_im, log_dt, b_re, b_im, c_re, c_im, T)
    w_state, w_out = _lane_tile_operators(mt, bst, cst, T, gl)
    a_tab = jnp.stack([jnp.stack([v.reshape(1, G * P) for v in aT[d]]) for d in range(2)])
    d_lane = jnp.tile(d_skip.reshape(GB, 1, LANES), (1, 1, T))

    KU, KS = T * LANES, 4 * gl * P
    ds = pl.pallas_call(
        functools.partial(_s5_state_in_body, T=T),
        out_shape=jax.ShapeDtypeStruct((2, 2, rows, G * P), F32),
        grid=(GB, rows // rt),
        in_specs=u_specs + [pl.BlockSpec((None, KU, KS), lambda gb, r: (gb, 0, 0))],
        out_specs=pl.BlockSpec((2, 2, rt, gl * P), lambda gb, r: (0, 0, r, gb)),
        compiler_params=_cparams("parallel", "arbitrary"),
        name="s5_state_in",
    )(*([u2] * T), w_state)

    lb = _pick(G * P, (1024, 512, 256, 128))
    s_start = pl.pallas_call(
        functools.partial(_s5_scan_body, B=B, nc_lat=nc_lat, nc_ctx=nc_ctx),
        out_shape=jax.ShapeDtypeStruct((2, 2, rows, G * P), F32),
        grid=(2, G * P // lb),
        in_specs=[pl.BlockSpec((None, 2, rows, lb), lambda d, j: (d, 0, 0, j)),
                  pl.BlockSpec((None, 2, 1, lb), lambda d, j: (d, 0, 0, j))],
        out_specs=pl.BlockSpec((None, 2, rows, lb), lambda d, j: (d, 0, 0, j)),
        compiler_params=_cparams("parallel", "parallel"),
        name="s5_scan",
    )(ds, a_tab)

    y3 = pl.pallas_call(
        functools.partial(_s5_out_body, T=T),
        out_shape=jax.ShapeDtypeStruct((T, rows, D), BF16),
        grid=(GB, rows // rt),
        in_specs=u_specs + [pl.BlockSpec((2, 2, rt, gl * P), lambda gb, r: (0, 0, r, gb)),
                            pl.BlockSpec((None, KU + KS, KU), lambda gb, r: (gb, 0, 0)),
                            pl.BlockSpec((None, 1, KU), lambda gb, r: (gb, 0, 0))],
        out_specs=pl.BlockSpec((T, rt, LANES), lambda gb, r: (0, r, gb)),
        compiler_params=_cparams("parallel", "arbitrary"),
        name="s5_out",
    )(*([u2] * T), s_start, w_out, d_lane)

    g2 = Geo(B, L, Lc, D, tm=T * 16)
    lhs = [(y3, pl.BlockSpec((T, g2.tm // T, D), lambda i, j: (0, i, 0)))]
    epi = [(xc, _row_tile(g2)), (modp, _mod_tile(g2, layer, 2))]
    (out,) = fused_matmul("s5_glu", g2, g2.tiles(True), lhs, w_glu.astype(BF16), 2, epi,
                          _untranspose_prologue, _glu_epilogue, [F32], alias={len(lhs) + 2: 0})
    return out


def _rope_tables(geo, head_dim):
    n_freq = head_dim // 4
    pos = jnp.arange(geo.L)
    inv_freq = ROPE_THETA ** (-jnp.arange(n_freq, dtype=F32) / n_freq)
    d = jnp.arange(LANES) % head_dim
    axis, half, f = d // (2 * n_freq), (d % (2 * n_freq)) // n_freq, d % n_freq
    p = jnp.where(axis[None, :] == 0, (pos // GRID_W)[:, None], (pos % GRID_W)[:, None]).astype(F32)
    ang = p * inv_freq[f][None, :]
    cos = jnp.cos(ang)
    sin = jnp.sin(ang) * jnp.where(half == 0, -1.0, 1.0)[None, :]
    rest = geo.R - geo.RL
    return (jnp.concatenate([jnp.tile(cos, (geo.B, 1)), jnp.ones((rest, LANES), F32)], axis=0),
            jnp.concatenate([jnp.tile(sin, (geo.B, 1)), jnp.zeros((rest, LANES), F32)], axis=0))


def _qkv_epilogue(accs, cos_ref, sin_ref, *, scale, n_freq):
    q, k, v = accs
    reps = q.shape[1] // LANES
    cos = jnp.concatenate([cos_ref[...]] * reps, axis=1)
    sin = jnp.concatenate([sin_ref[...]] * reps, axis=1)
    tn = q.shape[1]
    lane = lax.broadcasted_iota(jnp.int32, (1, tn), 1)
    first_half = (lane % (2 * n_freq)) < n_freq

    def rope(x):
        partner = jnp.where(first_half, pltpu.roll(x, tn - n_freq, axis=1), pltpu.roll(x, n_freq, axis=1))
        return x * cos + partner * sin

    return [rope(q) * scale, rope(k), v]


def _attn_body(q_ref, kl_ref, kc_ref, vl_ref, vc_ref, lam_ref, g_ref, o_ref, k_all, v_aug, *, lambda_init):
    q = q_ref[...]
    hw = q.shape[1]
    head_dim = hw // 2
    L = kl_ref.shape[0]
    lane = lax.broadcasted_iota(jnp.int32, (1, hw), 1)
    nt = (((1,), (1,)), ((), ()))

    @pl.when(pl.program_id(2) == 0)
    def _():
        k_all[:L, :] = kl_ref[...]
        k_all[L:, :] = kc_ref[...]
        v_aug[:L, :hw] = vl_ref[...]
        v_aug[L:, :hw] = vc_ref[...]
        v_aug[:, hw:] = jnp.ones((v_aug.shape[0], hw), v_aug.dtype)

    def attend(qm):
        s = lax.dot_general(qm, k_all[...], nt, preferred_element_type=F32)
        p = jnp.exp2(s - jnp.max(s, axis=1, keepdims=True)).astype(BF16)
        o = jnp.dot(p, v_aug[...], preferred_element_type=F32)
        return o[:, :hw] / o[:, hw:hw + 1]

    lp = lam_ref[...]
    lam = (jnp.exp(jnp.sum(lp[0:1] * lp[1:2], axis=1, keepdims=True))
           - jnp.exp(jnp.sum(lp[2:3] * lp[3:4], axis=1, keepdims=True)) + lambda_init)
    rows = q.shape[0] // ATTN_SUBTILES
    for r in range(ATTN_SUBTILES):
        qr = q[r * rows:(r + 1) * rows]
        o = (attend(jnp.where(lane < head_dim, qr, jnp.zeros_like(qr)))
             - lam * attend(jnp.where(lane >= head_dim, qr, jnp.zeros_like(qr))))
        o = o * lax.rsqrt(jnp.mean(o * o, axis=-1, keepdims=True) + EPS) * g_ref[...]
        o_ref[r * rows:(r + 1) * rows, :] = (o * (1.0 - lambda_init)).astype(o_ref.dtype)


def attn_mixer(geo, xc, modp, layer, gain, w_qkv, lam_params, subln_g, w_o):
    B, L, Lc, D = geo.B, geo.L, geo.Lc, geo.D
    head_dim = lam_params.shape[1]
    hw = 2 * head_dim
    assert hw == LANES and geo.R % Lc == 0
    H = D // hw
    lambda_init = 0.8 - 0.6 * math.exp(-0.3 * layer)
    cos, sin = _rope_tables(geo, head_dim)
    tab = pl.BlockSpec((geo.tm, LANES), lambda i, j: (i, 0))
    q, k, v = fused_matmul(
        "attn_qkv", geo, geo.tiles(True), _norm_mod_inputs(geo, xc, gain, modp, layer, 0),
        w_qkv.astype(BF16), 3, [(cos, tab), (sin, tab)], _norm_mod,
        functools.partial(_qkv_epilogue, scale=head_dim ** -0.5 * math.log2(math.e),
                          n_freq=head_dim // 4), [BF16] * 3)
    tq = _pick(L, (1024, 512, 256, 128, 64))
    nq = L // tq
    ctx_blk = geo.RL // Lc
    lat_kv = pl.BlockSpec((L, hw), lambda b, h, i: (b, h))
    ctx_kv = pl.BlockSpec((Lc, hw), lambda b, h, i: (ctx_blk + b, h))
    o = pl.pallas_call(
        functools.partial(_attn_body, lambda_init=lambda_init),
        out_shape=jax.ShapeDtypeStruct((geo.R, D), BF16),
        grid=(B, H, nq),
        in_specs=[pl.BlockSpec((tq, hw), lambda b, h, i: (b * nq + i, h)),
                  lat_kv, ctx_kv, lat_kv, ctx_kv,
                  pl.BlockSpec((4, head_dim), lambda b, h, i: (0, 0)),
                  pl.BlockSpec((1, hw), lambda b, h, i: (0, 0))],
        out_specs=pl.BlockSpec((tq, hw), lambda b, h, i: (b * nq + i, h)),
        scratch_shapes=[pltpu.VMEM((L + Lc, hw), BF16), pltpu.VMEM((L + Lc, 2 * hw), BF16)],
        compiler_params=_cparams("parallel", "parallel", "arbitrary"),
        name="diff_attention",
    )(q, k, k, v, v, lam_params, subln_g.reshape(1, hw))
    lhs = [(o, _row_full(geo, D))]
    epi = [(xc, _row_tile(geo)), (modp, _mod_tile(geo, layer, 2))]
    (out,) = fused_matmul("attn_out", geo, geo.tiles(False), lhs, w_o.astype(BF16), 1, epi,
                          _identity_prologue, _residual_epilogue, [F32], alias={len(lhs) + 1: 0})
    return out


def _router_body(x_ref, g_ref, sh_ref, sc_ref, rt_ref, hf_ref, aff_ref):
    h = _norm_mod(x_ref, g_ref, sh_ref, sc_ref)
    half = h.shape[1] // 2
    hb = h.astype(BF16)
    hbf = hb.astype(F32)
    hi = pltpu.bitcast(hbf[:, :half], jnp.uint32)
    lo = lax.shift_right_logical(pltpu.bitcast(hbf[:, half:], jnp.uint32), jnp.uint32(16))
    hf_ref[...] = hi | lo
    h_lo = (h - hbf).astype(BF16)
    rt = rt_ref[...]
    r_hi = rt.astype(BF16)
    r_lo = (rt - r_hi.astype(F32)).astype(BF16)
    nt = (((1,), (1,)), ((), ()))
    logits = (lax.dot_general(r_hi, hb, nt, preferred_element_type=F32)
              + lax.dot_general(r_hi, h_lo, nt, preferred_element_type=F32)
              + lax.dot_general(r_lo, hb, nt, preferred_element_type=F32))
    m = jnp.max(logits, axis=0, keepdims=True)
    e = jnp.exp(logits - m)
    aff_ref[...] = e / jnp.sum(e, axis=0, keepdims=True)


def moe_router(geo, xc, modp, layer, gain, router, with_ctx):
    D, tm = geo.D, geo.tm
    E = router.shape[1]
    i_only = lambda f: (lambda i: f(i, 0))
    specs = [s for _, s in _norm_mod_inputs(geo, xc, gain, modp, layer, 3)]
    specs = [pl.BlockSpec(s.block_shape, i_only(s.index_map)) for s in specs]
    return pl.pallas_call(
        _router_body,
        out_shape=[jax.ShapeDtypeStruct((geo.R, D // 2), jnp.uint32),
                   jax.ShapeDtypeStruct((E, geo.R), F32)],
        grid=(geo.tiles(with_ctx),),
        in_specs=specs + [pl.BlockSpec((E, D), lambda i: (0, 0))],
        out_specs=[pl.BlockSpec((tm, D // 2), lambda i: (i, 0)),
                   pl.BlockSpec((E, tm), lambda i: (0, i))],
        compiler_params=_cparams("parallel"),
        name="moe_router",
    )(xc, gain.reshape(1, D), modp, modp, router.T)


def _prefix_count(x, tri):
    n = x.shape[1]
    off = jnp.zeros((x.shape[0], 1), F32)
    parts = []
    for j in range(n // LANES):
        blk = x[:, j * LANES:(j + 1) * LANES]
        inc = jnp.dot(blk.astype(BF16), tri, preferred_element_type=F32)
        parts.append(inc - blk + off)
        off = off + inc[:, LANES - 1:LANES]
    return jnp.concatenate(parts, axis=1)


def _select_body(aff_ref, idx_ref, g_ref, *, cap):
    a = aff_ref[...]
    E, n = a.shape
    bits = pltpu.bitcast(a, jnp.int32)
    thr = jnp.zeros((E, 1), jnp.int32)
    for k in range(30, -1, -1):
        cand = thr | jnp.int32(1 << k)
        cnt = jnp.sum((bits >= cand).astype(F32), axis=1, keepdims=True)
        thr = jnp.where(cnt >= cap, cand, thr)
    gt = bits > thr
    eq = bits == thr
    need = cap - jnp.sum(gt.astype(F32), axis=1, keepdims=True)
    r = lax.broadcasted_iota(jnp.int32, (LANES, LANES), 0)
    c = lax.broadcasted_iota(jnp.int32, (LANES, LANES), 1)
    tri = (r <= c).astype(BF16)
    sel = gt | (eq & (_prefix_count(eq.astype(F32), tri) < need))
    self32 = sel.astype(F32)
    slot = _prefix_count(self32, tri)
    slot = jnp.where(sel, slot, -1.0)
    t = lax.broadcasted_iota(jnp.int32, (1, n), 1)
    t_hi = (t // 64).astype(F32)
    t_lo = (t % 64).astype(F32)
    a1 = a.astype(BF16).astype(F32)
    a2 = (a - a1).astype(BF16).astype(F32)
    a3 = (a - a1 - a2).astype(BF16).astype(F32)
    s_iota = lax.broadcasted_iota(jnp.int32, (cap, 1), 0).astype(F32)
    row = lax.broadcasted_iota(jnp.int32, (SUBLANES, 1), 0)
    nt = (((1,), (1,)), ((), ()))
    for e in range(E):
        onehot = jnp.where(slot[e:e + 1, :] == s_iota, 1.0, 0.0).astype(BF16)
        lhs = jnp.where(row == 0, t_hi, jnp.where(row == 1, t_lo, jnp.where(
            row == 2, a1[e:e + 1, :], jnp.where(row == 3, a2[e:e + 1, :], jnp.where(
                row == 4, a3[e:e + 1, :], 0.0))))).astype(BF16)
        res = lax.dot_general(lhs, onehot, nt, preferred_element_type=F32)
        idx_ref[e:e + 1, :] = (res[0:1, :] * 64.0 + res[1:2, :]).astype(jnp.int32)
        g_ref[e:e + 1, :] = res[2:3, :] + res[3:4, :] + res[4:5, :]


def moe_select(aff, n_samples, n, col_off):
    E = aff.shape[0]
    cap = max(1, CAPACITY_FACTOR * n // E)
    blk0 = col_off // n
    return pl.pallas_call(
        functools.partial(_select_body, cap=cap),
        out_shape=[jax.ShapeDtypeStruct((n_samples, E, cap), jnp.int32),
                   jax.ShapeDtypeStruct((n_samples, E, cap), F32)],
        grid=(n_samples,),
        in_specs=[pl.BlockSpec((E, n), lambda s: (0, blk0 + s))],
        out_specs=[pl.BlockSpec((None, E, cap), lambda s: (s, 0, 0)),
                   pl.BlockSpec((None, E, cap), lambda s: (s, 0, 0))],
        compiler_params=_cparams("parallel"),
        name="moe_select",
    )(aff)


def _gather_body(idx_ref, hf_ref, prev_ref, o_ref, *, cap, n_exp):
    del prev_ref
    base = (pl.program_id(0) * n_exp + pl.program_id(1)) * cap

    def step(i, carry):
        i0 = pl.multiple_of(i * SUBLANES, SUBLANES)
        rows = [hf_ref[pl.ds(idx_ref[base + i0 + k], 1), :] for k in range(SUBLANES)]
        for k in range(SUBLANES):
            o_ref[pl.ds(i0 + k, 1), :] = rows[k]
        return carry

    lax.fori_loop(0, cap // SUBLANES, step, 0)


def moe_gather(idx, hf, xg, n, row_off, slot_off):
    S, E, cap = idx.shape
    W = hf.shape[1]
    rb, sb = row_off // n, slot_off // cap
    grid_spec = pltpu.PrefetchScalarGridSpec(
        num_scalar_prefetch=1,
        grid=(S, E),
        in_specs=[pl.BlockSpec((n, W), lambda s, e, idx: (rb + s, 0)),
                  pl.BlockSpec(memory_space=pl.ANY)],
        out_specs=pl.BlockSpec((None, cap, W), lambda s, e, idx: (e, sb + s, 0)),
    )
    return pl.pallas_call(
        functools.partial(_gather_body, cap=cap, n_exp=E),
        out_shape=jax.ShapeDtypeStruct(xg.shape, xg.dtype),
        grid_spec=grid_spec,
        input_output_aliases={2: 0},
        compiler_params=_cparams("parallel", "arbitrary"),
        name="moe_gather",
    )(idx.reshape(-1), hf, xg)


def _expert_body(xg_ref, g_ref, wg_ref, wu_ref, wd_ref, y_ref):
    u = xg_ref[...]
    half = u.shape[1]
    a = pltpu.bitcast(u & jnp.uint32(0xFFFF0000), F32).astype(BF16)
    b = pltpu.bitcast(lax.shift_left(u, jnp.uint32(16)), F32).astype(BF16)
    gate = (jnp.dot(a, wg_ref[:half, :], preferred_element_type=F32)
            + jnp.dot(b, wg_ref[half:, :], preferred_element_type=F32))
    up = (jnp.dot(a, wu_ref[:half, :], preferred_element_type=F32)
          + jnp.dot(b, wu_ref[half:, :], preferred_element_type=F32))
    act = (gate * jax.nn.sigmoid(gate) * up).astype(BF16)
    y_ref[...] = jnp.dot(act, wd_ref[...], preferred_element_type=F32) * g_ref[...]


def moe_experts(xg, g_col, rows, w_gate, w_up, w_down):
    E, rows_alloc, W = xg.shape
    D, F = w_gate.shape[1], w_gate.shape[2]
    tr = next(t for t in (512, 256, 128, 64, 32, 16)
              if rows_alloc % t == 0 and (-rows % t) * 8 <= rows)
    return pl.pallas_call(
        _expert_body,
        out_shape=jax.ShapeDtypeStruct((E, rows_alloc, D), F32),
        grid=(E, -(-rows // tr)),
        in_specs=[pl.BlockSpec((None, tr, W), lambda e, r: (e, r, 0)),
                  pl.BlockSpec((None, tr, 1), lambda e, r: (e, r, 0)),
                  pl.BlockSpec((None, D, F), lambda e, r: (e, 0, 0)),
                  pl.BlockSpec((None, D, F), lambda e, r: (e, 0, 0)),
                  pl.BlockSpec((None, F, D), lambda e, r: (e, 0, 0))],
        out_specs=pl.BlockSpec((None, tr, D), lambda e, r: (e, r, 0)),
        compiler_params=_cparams("parallel", "arbitrary"),
        name="moe_experts",
    )(xg, g_col, w_gate, w_up, w_down)


def _combine_body(idx_ref, y_ref, x_ref, gate_ref, o_ref, *, cap, n_exp):
    e = pl.program_id(2)
    base = (pl.program_id(0) * n_exp + e) * cap

    @pl.when(e == 0)
    def _():
        o_ref[...] = jnp.zeros_like(o_ref)

    def step(i, carry):
        i0 = pl.multiple_of(i * SUBLANES, SUBLANES)
        y8 = y_ref[pl.ds(i0, SUBLANES), :]
        tok = [idx_ref[base + i0 + k] for k in range(SUBLANES)]
        acc = [o_ref[pl.ds(tok[k], 1), :] for k in range(SUBLANES)]
        for k in range(SUBLANES):
            o_ref[pl.ds(tok[k], 1), :] = acc[k] + y8[k:k + 1, :]
        return carry

    lax.fori_loop(0, cap // SUBLANES, step, 0)

    @pl.when(e == n_exp - 1)
    def _():
        o_ref[...] = x_ref[...] + gate_ref[...] * o_ref[...]


def moe_combine(geo, idx, y, xc, modp, layer, n, row_off, slot_off, mod_ctx):
    S, E, cap = idx.shape
    D = geo.D
    dc = _pick(D, (512, 256, 128))
    rb, sb = row_off // n, slot_off // cap
    mod_row = (lambda s: geo.B) if mod_ctx else (lambda s: s)
    grid_spec = pltpu.PrefetchScalarGridSpec(
        num_scalar_prefetch=1,
        grid=(S, D // dc, E),
        in_specs=[pl.BlockSpec((None, cap, dc), lambda s, c, e, idx: (e, sb + s, c)),
                  pl.BlockSpec((n, dc), lambda s, c, e, idx: (rb + s, c)),
                  pl.BlockSpec((None, None, None, 1, dc),
                               lambda s, c, e, idx: (layer, mod_row(s), 5, 0, c))],
        out_specs=pl.BlockSpec((n, dc), lambda s, c, e, idx: (rb + s, c)),
    )
    return pl.pallas_call(
        functools.partial(_combine_body, cap=cap, n_exp=E),
        out_shape=jax.ShapeDtypeStruct(xc.shape, xc.dtype),
        grid_spec=grid_spec,
        input_output_aliases={2: 0},
        compiler_params=_cparams("parallel", "parallel", "arbitrary"),
        name="moe_combine",
    )(idx.reshape(-1), y, xc, modp)


def moe_layer(geo, xc, modp, layer, gain, router, w_gate, w_up, w_down, with_ctx):
    B, L, Lc = geo.B, geo.L, geo.Lc
    E = router.shape[1]
    hf, aff = moe_router(geo, xc, modp, layer, gain, router, with_ctx)
    groups = [(L, 0)] + ([(Lc, geo.RL)] if with_ctx else [])
    sel = [moe_select(aff, B, n, off) for n, off in groups]
    slot_offs = [0, B * sel[0][0].shape[2]]
    rows = sum(B * s[0].shape[2] for s in sel)
    cap_l = sel[0][0].shape[2]
    rows_alloc = -(-rows // cap_l) * cap_l
    xg = jnp.zeros((E, rows_alloc, geo.D // 2), jnp.uint32)
    for (n, off), (idx, _), so in zip(groups, sel, slot_offs):
        xg = moe_gather(idx, hf, xg, n, off, so)
    g_col = jnp.concatenate([g.transpose(1, 0, 2).reshape(E, -1) for _, g in sel]
                            + [jnp.zeros((E, rows_alloc - rows), F32)], axis=1)[:, :, None]
    y = moe_experts(xg, g_col, rows, w_gate.astype(BF16), w_up.astype(BF16), w_down.astype(BF16))
    for k, ((n, off), (idx, _), so) in enumerate(zip(groups, sel, slot_offs)):
        xc = moe_combine(geo, idx, y, xc, modp, layer, n, off, so, mod_ctx=k == 1)
    return xc


def _final_norm_body(x_ref, g_ref, o_ref):
    x = x_ref[...]
    o_ref[...] = x * lax.rsqrt(jnp.mean(x * x, axis=-1, keepdims=True) + EPS) * g_ref[...]


def final_norm(geo, xc, gain):
    D, tm = geo.D, geo.tm
    return pl.pallas_call(
        _final_norm_body,
        out_shape=jax.ShapeDtypeStruct((geo.RL, D), F32),
        grid=(geo.RL // tm,),
        in_specs=[pl.BlockSpec((tm, D), lambda i: (i, 0)), pl.BlockSpec((1, D), lambda i: (0, 0))],
        out_specs=pl.BlockSpec((tm, D), lambda i: (i, 0)),
        compiler_params=_cparams("parallel"),
        name="final_norm",
    )(xc, gain.reshape(1, D))


def kernel(x, c, ctx, c_ctx, ada_w, ada_b, norm_mix_g, norm_ffn_g, final_norm_g, conv_w_in, conv_w, conv_w_out, ssm_w_in, ssm_a_re, ssm_a_im, ssm_log_dt, ssm_b_re, ssm_b_im, ssm_c_re, ssm_c_im, ssm_d, ssm_w_glu, attn_w_qkv, attn_lambda, attn_subln_g, attn_w_o, moe_router, moe_w_gate, moe_w_up, moe_w_down):
    B, L, D = x.shape
    Lc = ctx.shape[1]
    depth = ada_w.shape[0]
    geo = Geo(B, L, Lc, D)
    xc = jnp.concatenate([x.reshape(B * L, D), ctx.reshape(B * Lc, D),
                          jnp.zeros((geo.R - geo.Rv, D), F32)], axis=0)
    c_all = jnp.zeros((SUBLANES, D), F32).at[:B].set(c).at[B].set(c_ctx)
    modp = ada_modulation(c_all, ada_w, ada_b).reshape(depth, SUBLANES, 6, 1, D)
    ctx_live = [any(MIXER_READS_CTX[j % N_MIXERS] for j in range(i + 1, depth)) for i in range(depth)]
    for i in range(depth):
        kind, slot = i % N_MIXERS, i // N_MIXERS
        ctx_out = ctx_live[i]
        if kind == 0:
            xc = conv_mixer(geo, xc, modp, i, norm_mix_g[i], conv_w_in[slot], conv_w[slot],
                            conv_w_out[slot], ctx_out)
        elif kind == 1:
            xc = s5_mixer(geo, xc, modp, i, norm_mix_g[i], ssm_w_in[slot], ssm_a_re[slot], ssm_a_im[slot],
                          ssm_log_dt[slot], ssm_b_re[slot], ssm_b_im[slot], ssm_c_re[slot], ssm_c_im[slot],
                          ssm_d[slot], ssm_w_glu[slot])
        else:
            xc = attn_mixer(geo, xc, modp, i, norm_mix_g[i], attn_w_qkv[slot], attn_lambda[slot],
                            attn_subln_g[slot], attn_w_o[slot])
        xc = moe_layer(geo, xc, modp, i, norm_ffn_g[i], moe_router[i], moe_w_gate[i],
                       moe_w_up[i], moe_w_down[i], ctx_out)
    return final_norm(geo, xc, final_norm_g).reshape(B, L, D)
```

```python
import functools
import math

import jax
import jax.numpy as jnp
from jax import lax
from jax.experimental import pallas as pl
from jax.experimental.pallas import tpu as pltpu

F32 = jnp.float32
BF16 = jnp.bfloat16
EPS = 1e-6
N_MIXERS = 3
MIXER_READS_CTX = (False, True, True)
GRID_W = 64
ROPE_THETA = 10000.0
CAPACITY_FACTOR = 2
SSM_CHUNK = 16
SUBLANES = 8
CONV_HALO = 16
ATTN_SUBTILES = 4
LANES = 128
VMEM_LIMIT = 56 * 1024 * 1024


def _cparams(*sem):
    return pltpu.CompilerParams(dimension_semantics=sem, vmem_limit_bytes=VMEM_LIMIT)


def _pick(n, cands):
    for c in cands:
        if n % c == 0:
            return c
    raise ValueError(f"no tile in {cands} divides {n}")


class Geo:
    def __init__(self, B, L, Lc, D, tm=None):
        self.B, self.L, self.Lc, self.D = B, L, Lc, D
        self.RL = B * L
        self.Rv = B * (L + Lc)
        self.R = -(-self.Rv // L) * L
        self.tm = tm or _pick(math.gcd(L, B * Lc), (512, 256, 128, 64, 32, 16, 8))
        assert L % self.tm == 0 and (B * Lc) % self.tm == 0
        self.tn = _pick(D, (1024, 512, 256, 128))

    def mod_row(self, i):
        return jnp.minimum((i * self.tm) // self.L, self.B)

    def tiles(self, with_ctx):
        return (self.Rv if with_ctx else self.RL) // self.tm


def _ada_body(c_ref, w_ref, b_ref, o_ref):
    c = c_ref[...]
    s = (c * jax.nn.sigmoid(c)).astype(BF16)
    o_ref[...] = jnp.dot(s, w_ref[...].astype(BF16), preferred_element_type=F32) + b_ref[...]


def ada_modulation(c_all, ada_w, ada_b):
    depth, D, N = ada_w.shape
    tn = _pick(N, (1024, 512, 256, 128))
    return pl.pallas_call(
        _ada_body,
        out_shape=jax.ShapeDtypeStruct((depth, SUBLANES, N), F32),
        grid=(depth, N // tn),
        in_specs=[pl.BlockSpec((SUBLANES, D), lambda l, j: (0, 0)),
                  pl.BlockSpec((None, D, tn), lambda l, j: (l, 0, j)),
                  pl.BlockSpec((None, 1, tn), lambda l, j: (l, 0, j))],
        out_specs=pl.BlockSpec((None, SUBLANES, tn), lambda l, j: (l, 0, j)),
        compiler_params=_cparams("parallel", "parallel"),
        name="ada_modulation",
    )(c_all, ada_w, ada_b.reshape(depth, 1, N))


def _mm_body(*refs, n_lhs, n_w, n_epi, prologue, epilogue):
    lhs = refs[:n_lhs]
    ws = refs[n_lhs:n_lhs + n_w]
    epi = refs[n_lhs + n_w:n_lhs + n_w + n_epi]
    outs = refs[n_lhs + n_w + n_epi:-1]
    hb = refs[-1]

    @pl.when(pl.program_id(1) == 0)
    def _():
        hb[...] = prologue(*lhs).astype(hb.dtype)

    h = hb[...]
    accs = [jnp.dot(h, w[...], preferred_element_type=F32) for w in ws]
    for o, v in zip(outs, epilogue(accs, *epi)):
        o[...] = v.astype(o.dtype).reshape(o.shape)


def fused_matmul(name, geo, n_tiles, lhs, w, n_split, epi, prologue, epilogue, outs, alias=None,
                 out_layout=None):
    K = w.shape[0]
    N = w.shape[1] // n_split
    tm, tn = geo.tm, geo.tn
    nj = N // tn
    w_specs = [pl.BlockSpec((K, tn), functools.partial(lambda i, j, s: (0, s * nj + j), s=s))
               for s in range(n_split)]
    arrays = [a for a, _ in lhs] + [w] * n_split + [a for a, _ in epi]
    specs = [s for _, s in lhs] + w_specs + [s for _, s in epi]
    body = functools.partial(_mm_body, n_lhs=len(lhs), n_w=n_split, n_epi=len(epi),
                             prologue=prologue, epilogue=epilogue)
    out_shape, out_spec = out_layout or ((geo.R, N), pl.BlockSpec((tm, tn), lambda i, j: (i, j)))
    res = pl.pallas_call(
        body,
        out_shape=[jax.ShapeDtypeStruct(out_shape, dt) for dt in outs],
        grid=(n_tiles, nj),
        in_specs=specs,
        out_specs=[out_spec for _ in outs],
        scratch_shapes=[pltpu.VMEM((tm, K), BF16)],
        input_output_aliases=alias or {},
        compiler_params=_cparams("parallel", "arbitrary"),
        name=name,
    )(*arrays)
    return res


def _row_full(geo, K):
    return pl.BlockSpec((geo.tm, K), lambda i, j: (i, 0))


def _row_tile(geo):
    return pl.BlockSpec((geo.tm, geo.tn), lambda i, j: (i, j))


def _vec_full(K):
    return pl.BlockSpec((1, K), lambda i, j: (0, 0))


def _mod_full(geo, layer, k):
    return pl.BlockSpec((None, None, None, 1, geo.D),
                        lambda i, j: (layer, geo.mod_row(i), k, 0, 0))


def _mod_tile(geo, layer, k):
    return pl.BlockSpec((None, None, None, 1, geo.tn),
                        lambda i, j: (layer, geo.mod_row(i), k, 0, j))


def _norm_mod(x_ref, g_ref, sh_ref, sc_ref):
    x = x_ref[...]
    y = x * lax.rsqrt(jnp.mean(x * x, axis=-1, keepdims=True) + EPS)
    return (y * g_ref[...]) * (1.0 + sc_ref[...]) + sh_ref[...]


def _norm_mod_inputs(geo, xc, gain, modp, layer, k_shift):
    return [(xc, _row_full(geo, geo.D)), (gain.reshape(1, geo.D), _vec_full(geo.D)),
            (modp, _mod_full(geo, layer, k_shift)), (modp, _mod_full(geo, layer, k_shift + 1))]


def _residual_epilogue(accs, x_ref, gate_ref):
    return [x_ref[...] + gate_ref[...] * accs[0]]


def _conv_in_epilogue(accs):
    gate_b, gate_c, v = accs
    return [gate_b, gate_c * v]


def _conv_prologue(gb_ref, z_ref, zp_ref, zn_ref, wc_ref, *, geo):
    tm = geo.tm
    z = z_ref[...].astype(F32)
    t = lax.broadcasted_iota(jnp.int32, (tm, 1), 0)
    row = pl.program_id(0) * tm + t
    is_lat = row < geo.RL
    pos = jnp.where(is_lat, row % geo.L, (row - geo.RL) % geo.Lc)
    first = pos == 0
    last = pos == jnp.where(is_lat, geo.L - 1, geo.Lc - 1)
    halo_last = zp_ref[CONV_HALO - 1:CONV_HALO, :].astype(F32)
    z_prev = jnp.where(t == 0, halo_last, pltpu.roll(z, 1, axis=0))
    z_prev = jnp.where(first, 0.0, z_prev)
    z_next = jnp.where(t == tm - 1, zn_ref[0:1, :].astype(F32), pltpu.roll(z, tm - 1, axis=0))
    z_next = jnp.where(last, 0.0, z_next)
    wc = wc_ref[...]
    conv = wc[0:1, :] * z_prev + wc[1:2, :] * z + wc[2:3, :] * z_next
    return gb_ref[...].astype(F32) * conv


def conv_mixer(geo, xc, modp, layer, gain, w_in, w_conv, w_out, with_ctx):
    D, tm = geo.D, geo.tm
    n_tiles = geo.tiles(with_ctx)
    gb, z = fused_matmul("conv_in", geo, n_tiles,
                         _norm_mod_inputs(geo, xc, gain, modp, layer, 0),
                         w_in.astype(BF16), 3, [], _norm_mod, _conv_in_epilogue, [BF16, BF16])
    r8 = tm // CONV_HALO
    last_blk = geo.R // CONV_HALO - 1
    halo_prev = pl.BlockSpec((CONV_HALO, D), lambda i, j: (jnp.maximum(i * r8 - 1, 0), 0))
    halo_next = pl.BlockSpec((CONV_HALO, D), lambda i, j: (jnp.minimum((i + 1) * r8, last_blk), 0))
    lhs = [(gb, _row_full(geo, D)), (z, _row_full(geo, D)), (z, halo_prev), (z, halo_next),
           (w_conv, pl.BlockSpec((3, D), lambda i, j: (0, 0)))]
    epi = [(xc, _row_tile(geo)), (modp, _mod_tile(geo, layer, 2))]
    (out,) = fused_matmul("conv_out", geo, n_tiles, lhs, w_out.astype(BF16), 1, epi,
                          functools.partial(_conv_prologue, geo=geo), _residual_epilogue, [F32],
                          alias={len(lhs) + 1: 0})
    return out


def _s5_operators(a_re, a_im, log_dt, b_re, b_im, c_re, c_im, T):
    G, P = a_re.shape[1:]
    Cg = b_re.shape[-1]
    k = jnp.arange(T + 1, dtype=F32)[:, None, None]
    mt = 0.0
    bst, cst, aT = [], [], []
    lag = jnp.arange(T)[:, None] - jnp.arange(T)[None, :]
    for d in range(2):
        dt = jnp.exp(log_dt[d])[:, None]
        mag = jnp.exp(k * dt * a_re[d])
        pw_r, pw_i = mag * jnp.cos(k * dt * a_im[d]), mag * jnp.sin(k * dt * a_im[d])
        den = a_re[d] * a_re[d] + a_im[d] * a_im[d]
        nr = (pw_r[1] - 1.0) * a_re[d] + pw_i[1] * a_im[d]
        ni = pw_i[1] * a_re[d] - (pw_r[1] - 1.0) * a_im[d]
        fr, fi = (nr / den)[..., None], (ni / den)[..., None]
        bb_r = fr * b_re[d] - fi * b_im[d]
        bb_i = fr * b_im[d] + fi * b_re[d]
        w_r = c_re[d][None] * pw_r[:T, :, None, :] - c_im[d][None] * pw_i[:T, :, None, :]
        w_i = c_re[d][None] * pw_i[:T, :, None, :] + c_im[d][None] * pw_r[:T, :, None, :]
        kk = (jnp.einsum('kgjp,gpc->gkjc', w_r, bb_r, precision='highest')
              - jnp.einsum('kgjp,gpc->gkjc', w_i, bb_i, precision='highest'))
        dlag = lag if d == 0 else -lag
        m = jnp.where((dlag >= 0)[None, :, :, None, None], kk[:, jnp.clip(dlag, 0, T - 1)], 0.0)
        mt = mt + m.transpose(0, 2, 4, 1, 3).reshape(G, T * Cg, T * Cg)
        e = jnp.arange(T - 1, -1, -1) if d == 0 else jnp.arange(T)
        er, ei = pw_r[e], pw_i[e]
        s_r = er[:, :, :, None] * bb_r[None] - ei[:, :, :, None] * bb_i[None]
        s_i = er[:, :, :, None] * bb_i[None] + ei[:, :, :, None] * bb_r[None]
        bst.append(tuple(v.transpose(1, 0, 3, 2).reshape(G, T * Cg, P) for v in (s_r, s_i)))
        e = jnp.arange(1, T + 1) if d == 0 else jnp.arange(T, 0, -1)
        er, ei = pw_r[e], pw_i[e]
        o_r = c_re[d][None] * er[:, :, None, :] - c_im[d][None] * ei[:, :, None, :]
        o_i = -(c_re[d][None] * ei[:, :, None, :] + c_im[d][None] * er[:, :, None, :])
        cst.append(tuple(v.transpose(1, 3, 0, 2).reshape(G, P, T * Cg) for v in (o_r, o_i)))
        aT.append((pw_r[T], pw_i[T]))
    return mt, bst, cst, aT


def _lane_tile_operators(mt, bst, cst, T, gl):
    G, TC, _ = mt.shape
    Cg = TC // T
    P = bst[0][0].shape[-1]
    GB = G // gl

    def stack_u_rows(m):
        n = m.shape[-1]
        return m.reshape(GB, gl, T, Cg, n).transpose(0, 2, 1, 3, 4).reshape(GB, T * gl * Cg, n)

    a_in = stack_u_rows(jnp.concatenate([m for d in range(2) for m in bst[d]], axis=2))
    a_intra = stack_u_rows(mt)
    a_read = jnp.stack([m.reshape(GB, gl * P, TC) for d in range(2) for m in cst[d]], axis=1)
    return a_in.astype(BF16), a_intra.astype(BF16), a_read.reshape(GB, 4 * gl * P, TC).astype(BF16)


def _spread_block_diag(a_ref, w_ref, row0, *, gl, row_blk, col_blk):
    n_rows, k = a_ref.shape
    n_cols = k * gl
    kr = lax.broadcasted_iota(jnp.int32, (k, n_cols), 0)
    kc = lax.broadcasted_iota(jnp.int32, (k, n_cols), 1)
    rep = jnp.where((kr // col_blk == kc // (gl * col_blk)) & (kr % col_blk == kc % col_blk),
                    1.0, 0.0).astype(a_ref.dtype)
    col_group = (lax.broadcasted_iota(jnp.int32, (1, n_cols), 1) // col_blk) % gl
    step = 512 if n_rows % 512 == 0 else n_rows
    for r0 in range(0, n_rows, step):
        row_group = ((lax.broadcasted_iota(jnp.int32, (step, 1), 0) + r0) // row_blk) % gl
        full = jnp.dot(a_ref[r0:r0 + step, :], rep, preferred_element_type=F32)
        w_ref[row0 + r0:row0 + r0 + step, :] = jnp.where(row_group == col_group, full, 0.0).astype(w_ref.dtype)


def _s5_state_in_body(*refs, T, gl, Cg, P):
    u_refs, a_ref, ds_ref, w_ref = refs[:T], refs[T], refs[T + 1], refs[T + 2]

    @pl.when(pl.program_id(1) == 0)
    def _():
        _spread_block_diag(a_ref, w_ref, 0, gl=gl, row_blk=Cg, col_blk=P)

    u = jnp.concatenate([r[...] for r in u_refs], axis=1)
    res = jnp.dot(u, w_ref[...], preferred_element_type=F32)
    w = res.shape[1] // 4
    for k in range(4):
        ds_ref[k // 2, k % 2] = res[:, k * w:(k + 1) * w]


def _s5_scan_body(ds_ref, a_ref, s_ref, *, B, nc_lat, nc_ctx):
    a_re, a_im = a_ref[0], a_ref[1]
    zero = jnp.zeros((B, ds_ref.shape[-1]), F32)

    def segment(first, n, reverse, state):
        def step(i, st):
            s_re, s_im = st
            k = first + ((n - 1 - i) if reverse else i)
            for b in range(B):
                s_ref[0, pl.ds(k + b * n, 1), :] = s_re[b:b + 1].astype(s_ref.dtype)
                s_ref[1, pl.ds(k + b * n, 1), :] = s_im[b:b + 1].astype(s_ref.dtype)
            d_re = jnp.concatenate([ds_ref[0, pl.ds(k + b * n, 1), :] for b in range(B)], axis=0)
            d_im = jnp.concatenate([ds_ref[1, pl.ds(k + b * n, 1), :] for b in range(B)], axis=0)
            return a_re * s_re - a_im * s_im + d_re, a_re * s_im + a_im * s_re + d_im
        return lax.fori_loop(0, n, step, state)

    for d in range(2):
        @pl.when(pl.program_id(0) == d)
        def _():
            st = segment(B * nc_lat, nc_ctx, d == 1, (zero, zero))
            segment(0, nc_lat, d == 1, st)


def _s5_out_body(*refs, T, gl, Cg, P):
    u_refs = refs[:T]
    s_ref, ai_ref, ar_ref, d_ref, y_ref, w_ref = refs[T:]

    @pl.when(pl.program_id(1) == 0)
    def _():
        _spread_block_diag(ai_ref, w_ref, 0, gl=gl, row_blk=Cg, col_blk=Cg)
        _spread_block_diag(ar_ref, w_ref, ai_ref.shape[0], gl=gl, row_blk=P, col_blk=Cg)

    u = jnp.concatenate([r[...] for r in u_refs], axis=1)
    s = jnp.concatenate([s_ref[d, part] for d in range(2) for part in range(2)], axis=1).astype(u.dtype)
    y = (jnp.dot(jnp.concatenate([u, s], axis=1), w_ref[...], preferred_element_type=F32)
         + u.astype(F32) * d_ref[...])
    z = jax.nn.gelu(y).astype(y_ref.dtype)
    for t in range(T):
        y_ref[t] = z[:, t * LANES:(t + 1) * LANES]


def _glu_epilogue(accs, x_ref, gate_ref):
    val, gate = accs
    return [x_ref[...] + gate_ref[...] * (val * jax.nn.sigmoid(gate))]


def _identity_prologue(z_ref):
    return z_ref[...]


def _untranspose_prologue(z_ref):
    T, cb, D = z_ref.shape
    v = z_ref[...].reshape(T * cb, D)
    r = lax.broadcasted_iota(jnp.int32, (T * cb, T * cb), 0)
    c = lax.broadcasted_iota(jnp.int32, (T * cb, T * cb), 1)
    perm = jnp.where(c == (r % T) * cb + r // T, 1.0, 0.0).astype(v.dtype)
    return jnp.dot(perm, v, preferred_element_type=F32)


def s5_mixer(geo, xc, modp, layer, gain, w_in, a_re, a_im, log_dt, b_re, b_im, c_re, c_im, d_skip, w_glu):
    B, L, Lc, D = geo.B, geo.L, geo.Lc, geo.D
    G, P = a_re.shape[1:]
    Cg, T = D // G, SSM_CHUNK
    gl = LANES // Cg
    GB = G // gl
    nc_ctx, nc_lat = Lc // T, L // T
    rows = geo.Rv // T
    rt = _pick(rows, (272, 256, 128, 64, 32, 16))
    (u,) = fused_matmul("s5_in", geo, geo.tiles(True), _norm_mod_inputs(geo, xc, gain, modp, layer, 0),
                        w_in.astype(BF16), 1, [], _norm_mod, lambda accs: accs, [BF16])
    u2 = u.reshape(geo.R // T, T * D)
    u_specs = [pl.BlockSpec((rt, LANES), functools.partial(lambda gb, r, t: (r, t * (D // LANES) + gb), t=t))
               for t in range(T)]

    mt, bst, cst, aT = _s5_operators(a_re, a_im, log_dt, b_re, b_im, c_re, c_im, T)
    a_in, a_intra, a_read = _lane_tile_operators(mt, bst, cst, T, gl)
    a_tab = jnp.stack([jnp.stack([v.reshape(1, G * P) for v in aT[d]]) for d in range(2)])
    d_lane = jnp.tile(d_skip.reshape(GB, 1, LANES), (1, 1, T))

    KU, KS = T * LANES, 4 * gl * P
    dims = dict(T=T, gl=gl, Cg=Cg, P=P)
    ds = pl.pallas_call(
        functools.partial(_s5_state_in_body, **dims),
        out_shape=jax.ShapeDtypeStruct((2, 2, rows, G * P), F32),
        grid=(GB, rows // rt),
        in_specs=u_specs + [pl.BlockSpec((None, KU, 4 * P), lambda gb, r: (gb, 0, 0))],
        out_specs=pl.BlockSpec((2, 2, rt, gl * P), lambda gb, r: (0, 0, r, gb)),
        scratch_shapes=[pltpu.VMEM((KU, KS), BF16)],
        compiler_params=_cparams("parallel", "arbitrary"),
        name="s5_state_in",
    )(*([u2] * T), a_in)

    lb = _pick(G * P, (1024, 512, 256, 128))
    s_start = pl.pallas_call(
        functools.partial(_s5_scan_body, B=B, nc_lat=nc_lat, nc_ctx=nc_ctx),
        out_shape=jax.ShapeDtypeStruct((2, 2, rows, G * P), F32),
        grid=(2, G * P // lb),
        in_specs=[pl.BlockSpec((None, 2, rows, lb), lambda d, j: (d, 0, 0, j)),
                  pl.BlockSpec((None, 2, 1, lb), lambda d, j: (d, 0, 0, j))],
        out_specs=pl.BlockSpec((None, 2, rows, lb), lambda d, j: (d, 0, 0, j)),
        compiler_params=_cparams("parallel", "parallel"),
        name="s5_scan",
    )(ds, a_tab)

    y3 = pl.pallas_call(
        functools.partial(_s5_out_body, **dims),
        out_shape=jax.ShapeDtypeStruct((T, rows, D), BF16),
        grid=(GB, rows // rt),
        in_specs=u_specs + [pl.BlockSpec((2, 2, rt, gl * P), lambda gb, r: (0, 0, r, gb)),
                            pl.BlockSpec((None, KU, T * Cg), lambda gb, r: (gb, 0, 0)),
                            pl.BlockSpec((None, KS, T * Cg), lambda gb, r: (gb, 0, 0)),
                            pl.BlockSpec((None, 1, KU), lambda gb, r: (gb, 0, 0))],
        out_specs=pl.BlockSpec((T, rt, LANES), lambda gb, r: (0, r, gb)),
        scratch_shapes=[pltpu.VMEM((KU + KS, KU), BF16)],
        compiler_params=_cparams("parallel", "arbitrary"),
        name="s5_out",
    )(*([u2] * T), s_start, a_intra, a_read, d_lane)

    g2 = Geo(B, L, Lc, D, tm=T * 16)
    lhs = [(y3, pl.BlockSpec((T, g2.tm // T, D), lambda i, j: (0, i, 0)))]
    epi = [(xc, _row_tile(g2)), (modp, _mod_tile(g2, layer, 2))]
    (out,) = fused_matmul("s5_glu", g2, g2.tiles(True), lhs, w_glu.astype(BF16), 2, epi,
                          _untranspose_prologue, _glu_epilogue, [F32], alias={len(lhs) + 2: 0})
    return out


def _rope_tables(geo, head_dim):
    n_freq = head_dim // 4
    pos = jnp.arange(geo.L)
    inv_freq = ROPE_THETA ** (-jnp.arange(n_freq, dtype=F32) / n_freq)
    d = jnp.arange(LANES) % head_dim
    axis, half, f = d // (2 * n_freq), (d % (2 * n_freq)) // n_freq, d % n_freq
    p = jnp.where(axis[None, :] == 0, (pos // GRID_W)[:, None], (pos % GRID_W)[:, None]).astype(F32)
    ang = p * inv_freq[f][None, :]
    cos = jnp.cos(ang)
    sin = jnp.sin(ang) * jnp.where(half == 0, -1.0, 1.0)[None, :]
    rest = geo.R - geo.RL
    return (jnp.concatenate([jnp.tile(cos, (geo.B, 1)), jnp.ones((rest, LANES), F32)], axis=0),
            jnp.concatenate([jnp.tile(sin, (geo.B, 1)), jnp.zeros((rest, LANES), F32)], axis=0))


def _qkv_epilogue(accs, cos_ref, sin_ref, *, scale, n_freq):
    q, k, v = accs
    reps = q.shape[1] // LANES
    cos = jnp.concatenate([cos_ref[...]] * reps, axis=1)
    sin = jnp.concatenate([sin_ref[...]] * reps, axis=1)
    tn = q.shape[1]
    lane = lax.broadcasted_iota(jnp.int32, (1, tn), 1)
    first_half = (lane % (2 * n_freq)) < n_freq

    def rope(x):
        partner = jnp.where(first_half, pltpu.roll(x, tn - n_freq, axis=1), pltpu.roll(x, n_freq, axis=1))
        return x * cos + partner * sin

    return [rope(q) * scale, rope(k), v]


def _attn_body(q_ref, kl_ref, kc_ref, vl_ref, vc_ref, lam_ref, g_ref, o_ref, k_all, v_aug, *, lambda_init):
    q = q_ref[...]
    hw = q.shape[1]
    head_dim = hw // 2
    L = kl_ref.shape[0]
    lane = lax.broadcasted_iota(jnp.int32, (1, hw), 1)
    nt = (((1,), (1,)), ((), ()))

    @pl.when(pl.program_id(2) == 0)
    def _():
        k_all[:L, :] = kl_ref[...]
        k_all[L:, :] = kc_ref[...]
        v_aug[:L, :hw] = vl_ref[...]
        v_aug[L:, :hw] = vc_ref[...]
        v_aug[:, hw:] = jnp.ones((v_aug.shape[0], hw), v_aug.dtype)

    def attend(qm):
        s = lax.dot_general(qm, k_all[...], nt, preferred_element_type=F32)
        p = jnp.exp2(s - jnp.max(s, axis=1, keepdims=True)).astype(BF16)
        o = jnp.dot(p, v_aug[...], preferred_element_type=F32)
        return o[:, :hw] / o[:, hw:hw + 1]

    lp = lam_ref[...]
    lam = (jnp.exp(jnp.sum(lp[0:1] * lp[1:2], axis=1, keepdims=True))
           - jnp.exp(jnp.sum(lp[2:3] * lp[3:4], axis=1, keepdims=True)) + lambda_init)
    rows = q.shape[0] // ATTN_SUBTILES
    for r in range(ATTN_SUBTILES):
        qr = q[r * rows:(r + 1) * rows]
        o = (attend(jnp.where(lane < head_dim, qr, jnp.zeros_like(qr)))
             - lam * attend(jnp.where(lane >= head_dim, qr, jnp.zeros_like(qr))))
        o = o * lax.rsqrt(jnp.mean(o * o, axis=-1, keepdims=True) + EPS) * g_ref[...]
        o_ref[r * rows:(r + 1) * rows, :] = (o * (1.0 - lambda_init)).astype(o_ref.dtype)


def attn_mixer(geo, xc, modp, layer, gain, w_qkv, lam_params, subln_g, w_o):
    B, L, Lc, D = geo.B, geo.L, geo.Lc, geo.D
    head_dim = lam_params.shape[1]
    hw = 2 * head_dim
    assert hw == LANES and geo.R % Lc == 0
    H = D // hw
    lambda_init = 0.8 - 0.6 * math.exp(-0.3 * layer)
    cos, sin = _rope_tables(geo, head_dim)
    tab = pl.BlockSpec((geo.tm, LANES), lambda i, j: (i, 0))
    q, k, v = fused_matmul(
        "attn_qkv", geo, geo.tiles(True), _norm_mod_inputs(geo, xc, gain, modp, layer, 0),
        w_qkv.astype(BF16), 3, [(cos, tab), (sin, tab)], _norm_mod,
        functools.partial(_qkv_epilogue, scale=head_dim ** -0.5 * math.log2(math.e),
                          n_freq=head_dim // 4), [BF16] * 3)
    tq = _pick(L, (1024, 512, 256, 128, 64))
    nq = L // tq
    ctx_blk = geo.RL // Lc
    lat_kv = pl.BlockSpec((L, hw), lambda b, h, i: (b, h))
    ctx_kv = pl.BlockSpec((Lc, hw), lambda b, h, i: (ctx_blk + b, h))
    o = pl.pallas_call(
        functools.partial(_attn_body, lambda_init=lambda_init),
        out_shape=jax.ShapeDtypeStruct((geo.R, D), BF16),
        grid=(B, H, nq),
        in_specs=[pl.BlockSpec((tq, hw), lambda b, h, i: (b * nq + i, h)),
                  lat_kv, ctx_kv, lat_kv, ctx_kv,
                  pl.BlockSpec((4, head_dim), lambda b, h, i: (0, 0)),
                  pl.BlockSpec((1, hw), lambda b, h, i: (0, 0))],
        out_specs=pl.BlockSpec((tq, hw), lambda b, h, i: (b * nq + i, h)),
        scratch_shapes=[pltpu.VMEM((L + Lc, hw), BF16), pltpu.VMEM((L + Lc, 2 * hw), BF16)],
        compiler_params=_cparams("parallel", "parallel", "arbitrary"),
        name="diff_attention",
    )(q, k, k, v, v, lam_params, subln_g.reshape(1, hw))
    lhs = [(o, _row_full(geo, D))]
    epi = [(xc, _row_tile(geo)), (modp, _mod_tile(geo, layer, 2))]
    (out,) = fused_matmul("attn_out", geo, geo.tiles(False), lhs, w_o.astype(BF16), 1, epi,
                          _identity_prologue, _residual_epilogue, [F32], alias={len(lhs) + 1: 0})
    return out


def _router_body(x_ref, g_ref, sh_ref, sc_ref, rt_ref, hf_ref, aff_ref):
    h = _norm_mod(x_ref, g_ref, sh_ref, sc_ref)
    half = h.shape[1] // 2
    hb = h.astype(BF16)
    hbf = hb.astype(F32)
    hi = pltpu.bitcast(hbf[:, :half], jnp.uint32)
    lo = lax.shift_right_logical(pltpu.bitcast(hbf[:, half:], jnp.uint32), jnp.uint32(16))
    hf_ref[...] = hi | lo
    h_lo = (h - hbf).astype(BF16)
    rt = rt_ref[...]
    r_hi = rt.astype(BF16)
    r_lo = (rt - r_hi.astype(F32)).astype(BF16)
    nt = (((1,), (1,)), ((), ()))
    logits = (lax.dot_general(r_hi, hb, nt, preferred_element_type=F32)
              + lax.dot_general(r_hi, h_lo, nt, preferred_element_type=F32)
              + lax.dot_general(r_lo, hb, nt, preferred_element_type=F32))
    m = jnp.max(logits, axis=0, keepdims=True)
    e = jnp.exp(logits - m)
    aff_ref[...] = e / jnp.sum(e, axis=0, keepdims=True)


def moe_router(geo, xc, modp, layer, gain, router, with_ctx):
    D, tm = geo.D, geo.tm
    E = router.shape[1]
    i_only = lambda f: (lambda i: f(i, 0))
    specs = [s for _, s in _norm_mod_inputs(geo, xc, gain, modp, layer, 3)]
    specs = [pl.BlockSpec(s.block_shape, i_only(s.index_map)) for s in specs]
    return pl.pallas_call(
        _router_body,
        out_shape=[jax.ShapeDtypeStruct((geo.R, D // 2), jnp.uint32),
                   jax.ShapeDtypeStruct((E, geo.R), F32)],
        grid=(geo.tiles(with_ctx),),
        in_specs=specs + [pl.BlockSpec((E, D), lambda i: (0, 0))],
        out_specs=[pl.BlockSpec((tm, D // 2), lambda i: (i, 0)),
                   pl.BlockSpec((E, tm), lambda i: (0, i))],
        compiler_params=_cparams("parallel"),
        name="moe_router",
    )(xc, gain.reshape(1, D), modp, modp, router.T)


def _prefix_count(x, tri):
    n = x.shape[1]
    off = jnp.zeros((x.shape[0], 1), F32)
    parts = []
    for j in range(n // LANES):
        blk = x[:, j * LANES:(j + 1) * LANES]
        inc = jnp.dot(blk.astype(BF16), tri, preferred_element_type=F32)
        parts.append(inc - blk + off)
        off = off + inc[:, LANES - 1:LANES]
    return jnp.concatenate(parts, axis=1)


def _select_body(aff_ref, idx_ref, g_ref, *, cap):
    a = aff_ref[...]
    E, n = a.shape
    bits = pltpu.bitcast(a, jnp.int32)
    thr = jnp.zeros((E, 1), jnp.int32)
    for k in range(30, -1, -1):
        cand = thr | jnp.int32(1 << k)
        cnt = jnp.sum((bits >= cand).astype(F32), axis=1, keepdims=True)
        thr = jnp.where(cnt >= cap, cand, thr)
    gt = bits > thr
    eq = bits == thr
    need = cap - jnp.sum(gt.astype(F32), axis=1, keepdims=True)
    r = lax.broadcasted_iota(jnp.int32, (LANES, LANES), 0)
    c = lax.broadcasted_iota(jnp.int32, (LANES, LANES), 1)
    tri = (r <= c).astype(BF16)
    sel = gt | (eq & (_prefix_count(eq.astype(F32), tri) < need))
    self32 = sel.astype(F32)
    slot = _prefix_count(self32, tri)
    slot = jnp.where(sel, slot, -1.0)
    t = lax.broadcasted_iota(jnp.int32, (1, n), 1)
    t_hi = (t // 64).astype(F32)
    t_lo = (t % 64).astype(F32)
    a1 = a.astype(BF16).astype(F32)
    a2 = (a - a1).astype(BF16).astype(F32)
    a3 = (a - a1 - a2).astype(BF16).astype(F32)
    s_iota = lax.broadcasted_iota(jnp.int32, (cap, 1), 0).astype(F32)
    row = lax.broadcasted_iota(jnp.int32, (SUBLANES, 1), 0)
    nt = (((1,), (1,)), ((), ()))
    for e in range(E):
        onehot = jnp.where(slot[e:e + 1, :] == s_iota, 1.0, 0.0).astype(BF16)
        lhs = jnp.where(row == 0, t_hi, jnp.where(row == 1, t_lo, jnp.where(
            row == 2, a1[e:e + 1, :], jnp.where(row == 3, a2[e:e + 1, :], jnp.where(
                row == 4, a3[e:e + 1, :], 0.0))))).astype(BF16)
        res = lax.dot_general(lhs, onehot, nt, preferred_element_type=F32)
        idx_ref[e:e + 1, :] = (res[0:1, :] * 64.0 + res[1:2, :]).astype(jnp.int32)
        g_ref[e:e + 1, :] = res[2:3, :] + res[3:4, :] + res[4:5, :]


def moe_select(aff, n_samples, n, col_off):
    E = aff.shape[0]
    cap = max(1, CAPACITY_FACTOR * n // E)
    blk0 = col_off // n
    return pl.pallas_call(
        functools.partial(_select_body, cap=cap),
        out_shape=[jax.ShapeDtypeStruct((n_samples, E, cap), jnp.int32),
                   jax.ShapeDtypeStruct((n_samples, E, cap), F32)],
        grid=(n_samples,),
        in_specs=[pl.BlockSpec((E, n), lambda s: (0, blk0 + s))],
        out_specs=[pl.BlockSpec((None, E, cap), lambda s: (s, 0, 0)),
                   pl.BlockSpec((None, E, cap), lambda s: (s, 0, 0))],
        compiler_params=_cparams("parallel"),
        name="moe_select",
    )(aff)


def _gather_body(idx_ref, hf_ref, prev_ref, o_ref, *, cap, n_exp):
    del prev_ref
    base = (pl.program_id(0) * n_exp + pl.program_id(1)) * cap

    def step(i, carry):
        i0 = pl.multiple_of(i * SUBLANES, SUBLANES)
        rows = [hf_ref[pl.ds(idx_ref[base + i0 + k], 1), :] for k in range(SUBLANES)]
        for k in range(SUBLANES):
            o_ref[pl.ds(i0 + k, 1), :] = rows[k]
        return carry

    lax.fori_loop(0, cap // SUBLANES, step, 0)


def moe_gather(idx, hf, xg, n, row_off, slot_off):
    S, E, cap = idx.shape
    W = hf.shape[1]
    rb, sb = row_off // n, slot_off // cap
    grid_spec = pltpu.PrefetchScalarGridSpec(
        num_scalar_prefetch=1,
        grid=(S, E),
        in_specs=[pl.BlockSpec((n, W), lambda s, e, idx: (rb + s, 0)),
                  pl.BlockSpec(memory_space=pl.ANY)],
        out_specs=pl.BlockSpec((None, cap, W), lambda s, e, idx: (e, sb + s, 0)),
    )
    return pl.pallas_call(
        functools.partial(_gather_body, cap=cap, n_exp=E),
        out_shape=jax.ShapeDtypeStruct(xg.shape, xg.dtype),
        grid_spec=grid_spec,
        input_output_aliases={2: 0},
        compiler_params=_cparams("parallel", "arbitrary"),
        name="moe_gather",
    )(idx.reshape(-1), hf, xg)


def _expert_body(xg_ref, g_ref, wg_ref, wu_ref, wd_ref, y_ref):
    u = xg_ref[...]
    half = u.shape[1]
    a = pltpu.bitcast(u & jnp.uint32(0xFFFF0000), F32).astype(BF16)
    b = pltpu.bitcast(lax.shift_left(u, jnp.uint32(16)), F32).astype(BF16)
    gate = (jnp.dot(a, wg_ref[:half, :], preferred_element_type=F32)
            + jnp.dot(b, wg_ref[half:, :], preferred_element_type=F32))
    up = (jnp.dot(a, wu_ref[:half, :], preferred_element_type=F32)
          + jnp.dot(b, wu_ref[half:, :], preferred_element_type=F32))
    act = (gate * jax.nn.sigmoid(gate) * up).astype(BF16)
    y_ref[...] = jnp.dot(act, wd_ref[...], preferred_element_type=F32) * g_ref[...]


def moe_experts(xg, g_col, rows, layer, w_gate, w_up, w_down):
    E, rows_alloc, W = xg.shape
    D, F = w_gate.shape[2], w_gate.shape[3]
    tr = next(t for t in (512, 256, 128, 64, 32, 16)
              if rows_alloc % t == 0 and (-rows % t) * 8 <= rows)
    return pl.pallas_call(
        _expert_body,
        out_shape=jax.ShapeDtypeStruct((E, rows_alloc, D), F32),
        grid=(E, -(-rows // tr)),
        in_specs=[pl.BlockSpec((None, tr, W), lambda e, r: (e, r, 0)),
                  pl.BlockSpec((None, tr, 1), lambda e, r: (e, r, 0)),
                  pl.BlockSpec((None, None, D, F), lambda e, r: (layer, e, 0, 0)),
                  pl.BlockSpec((None, None, D, F), lambda e, r: (layer, e, 0, 0)),
                  pl.BlockSpec((None, None, F, D), lambda e, r: (layer, e, 0, 0))],
        out_specs=pl.BlockSpec((None, tr, D), lambda e, r: (e, r, 0)),
        compiler_params=_cparams("parallel", "arbitrary"),
        name="moe_experts",
    )(xg, g_col, w_gate, w_up, w_down)


def _combine_body(idx_ref, y_ref, x_ref, gate_ref, o_ref, *, cap, n_exp):
    e = pl.program_id(2)
    base = (pl.program_id(0) * n_exp + e) * cap

    @pl.when(e == 0)
    def _():
        o_ref[...] = jnp.zeros_like(o_ref)

    def step(i, carry):
        i0 = pl.multiple_of(i * SUBLANES, SUBLANES)
        y8 = y_ref[pl.ds(i0, SUBLANES), :]
        tok = [idx_ref[base + i0 + k] for k in range(SUBLANES)]
        acc = [o_ref[pl.ds(tok[k], 1), :] for k in range(SUBLANES)]
        for k in range(SUBLANES):
            o_ref[pl.ds(tok[k], 1), :] = acc[k] + y8[k:k + 1, :]
        return carry

    lax.fori_loop(0, cap // SUBLANES, step, 0)

    @pl.when(e == n_exp - 1)
    def _():
        o_ref[...] = x_ref[...] + gate_ref[...] * o_ref[...]


def moe_combine(geo, idx, y, xc, modp, layer, n, row_off, slot_off, mod_ctx):
    S, E, cap = idx.shape
    D = geo.D
    dc = _pick(D, (512, 256, 128))
    rb, sb = row_off // n, slot_off // cap
    mod_row = (lambda s: geo.B) if mod_ctx else (lambda s: s)
    grid_spec = pltpu.PrefetchScalarGridSpec(
        num_scalar_prefetch=1,
        grid=(S, D // dc, E),
        in_specs=[pl.BlockSpec((None, cap, dc), lambda s, c, e, idx: (e, sb + s, c)),
                  pl.BlockSpec((n, dc), lambda s, c, e, idx: (rb + s, c)),
                  pl.BlockSpec((None, None, None, 1, dc),
                               lambda s, c, e, idx: (layer, mod_row(s), 5, 0, c))],
        out_specs=pl.BlockSpec((n, dc), lambda s, c, e, idx: (rb + s, c)),
    )
    return pl.pallas_call(
        functools.partial(_combine_body, cap=cap, n_exp=E),
        out_shape=jax.ShapeDtypeStruct(xc.shape, xc.dtype),
        grid_spec=grid_spec,
        input_output_aliases={2: 0},
        compiler_params=_cparams("parallel", "parallel", "arbitrary"),
        name="moe_combine",
    )(idx.reshape(-1), y, xc, modp)


def moe_layer(geo, xc, modp, layer, gain, router, w_gate, w_up, w_down, with_ctx):
    B, L, Lc = geo.B, geo.L, geo.Lc
    E = router.shape[1]
    hf, aff = moe_router(geo, xc, modp, layer, gain, router, with_ctx)
    groups = [(L, 0)] + ([(Lc, geo.RL)] if with_ctx else [])
    sel = [moe_select(aff, B, n, off) for n, off in groups]
    slot_offs = [0, B * sel[0][0].shape[2]]
    rows = sum(B * s[0].shape[2] for s in sel)
    cap_l = sel[0][0].shape[2]
    rows_alloc = -(-rows // cap_l) * cap_l
    xg = jnp.zeros((E, rows_alloc, geo.D // 2), jnp.uint32)
    for (n, off), (idx, _), so in zip(groups, sel, slot_offs):
        xg = moe_gather(idx, hf, xg, n, off, so)
    g_col = jnp.concatenate([g.transpose(1, 0, 2).reshape(E, -1) for _, g in sel]
                            + [jnp.zeros((E, rows_alloc - rows), F32)], axis=1)[:, :, None]
    y = moe_experts(xg, g_col, rows, layer, w_gate, w_up, w_down)
    for k, ((n, off), (idx, _), so) in enumerate(zip(groups, sel, slot_offs)):
        xc = moe_combine(geo, idx, y, xc, modp, layer, n, off, so, mod_ctx=k == 1)
    return xc


def _final_norm_body(x_ref, g_ref, o_ref):
    x = x_ref[...]
    o_ref[...] = x * lax.rsqrt(jnp.mean(x * x, axis=-1, keepdims=True) + EPS) * g_ref[...]


def final_norm(geo, xc, gain):
    D, tm = geo.D, geo.tm
    return pl.pallas_call(
        _final_norm_body,
        out_shape=jax.ShapeDtypeStruct((geo.RL, D), F32),
        grid=(geo.RL // tm,),
        in_specs=[pl.BlockSpec((tm, D), lambda i: (i, 0)), pl.BlockSpec((1, D), lambda i: (0, 0))],
        out_specs=pl.BlockSpec((tm, D), lambda i: (i, 0)),
        compiler_params=_cparams("parallel"),
        name="final_norm",
    )(xc, gain.reshape(1, D))


def kernel(x, c, ctx, c_ctx, ada_w, ada_b, norm_mix_g, norm_ffn_g, final_norm_g, conv_w_in, conv_w, conv_w_out, ssm_w_in, ssm_a_re, ssm_a_im, ssm_log_dt, ssm_b_re, ssm_b_im, ssm_c_re, ssm_c_im, ssm_d, ssm_w_glu, attn_w_qkv, attn_lambda, attn_subln_g, attn_w_o, moe_router, moe_w_gate, moe_w_up, moe_w_down):
    B, L, D = x.shape
    Lc = ctx.shape[1]
    depth = ada_w.shape[0]
    geo = Geo(B, L, Lc, D)
    xc = jnp.concatenate([x.reshape(B * L, D), ctx.reshape(B * Lc, D),
                          jnp.zeros((geo.R - geo.Rv, D), F32)], axis=0)
    c_all = jnp.zeros((SUBLANES, D), F32).at[:B].set(c).at[B].set(c_ctx)
    modp = ada_modulation(c_all, ada_w, ada_b).reshape(depth, SUBLANES, 6, 1, D)
    ctx_live = [any(MIXER_READS_CTX[j % N_MIXERS] for j in range(i + 1, depth)) for i in range(depth)]
    w_gate, w_up, w_down = moe_w_gate.astype(BF16), moe_w_up.astype(BF16), moe_w_down.astype(BF16)
    for i in range(depth):
        kind, slot = i % N_MIXERS, i // N_MIXERS
        ctx_out = ctx_live[i]
        if kind == 0:
            xc = conv_mixer(geo, xc, modp, i, norm_mix_g[i], conv_w_in[slot], conv_w[slot],
                            conv_w_out[slot], ctx_out)
        elif kind == 1:
            xc = s5_mixer(geo, xc, modp, i, norm_mix_g[i], ssm_w_in[slot], ssm_a_re[slot], ssm_a_im[slot],
                          ssm_log_dt[slot], ssm_b_re[slot], ssm_b_im[slot], ssm_c_re[slot], ssm_c_im[slot],
                          ssm_d[slot], ssm_w_glu[slot])
        else:
            xc = attn_mixer(geo, xc, modp, i, norm_mix_g[i], attn_w_qkv[slot], attn_lambda[slot],
                            attn_subln_g[slot], attn_w_o[slot])
        xc = moe_layer(geo, xc, modp, i, norm_ffn_g[i], moe_router[i], w_gate, w_up, w_down, ctx_out)
    return final_norm(geo, xc, final_norm_g).reshape(B, L, D)
```

```python
import functools
import math

import jax
import jax.numpy as jnp
from jax import lax
from jax.experimental import pallas as pl
from jax.experimental.pallas import tpu as pltpu

F32 = jnp.float32
BF16 = jnp.bfloat16
EPS = 1e-6
N_MIXERS = 3
MIXER_READS_CTX = (False, True, True)
GRID_W = 64
ROPE_THETA = 10000.0
CAPACITY_FACTOR = 2
SSM_CHUNK = 16
SUBLANES = 8
CONV_HALO = 16
ATTN_SUBTILES = 8
LANES = 128
VMEM_LIMIT = 56 * 1024 * 1024


def _cparams(*sem):
    return pltpu.CompilerParams(dimension_semantics=sem, vmem_limit_bytes=VMEM_LIMIT)


def _pick(n, cands):
    for c in cands:
        if n % c == 0:
            return c
    raise ValueError(f"no tile in {cands} divides {n}")


class Geo:
    def __init__(self, B, L, Lc, D, tm=None):
        self.B, self.L, self.Lc, self.D = B, L, Lc, D
        self.RL = B * L
        self.Rv = B * (L + Lc)
        self.R = -(-self.Rv // L) * L
        self.tm = tm or _pick(math.gcd(L, B * Lc), (512, 256, 128, 64, 32, 16, 8))
        assert L % self.tm == 0 and (B * Lc) % self.tm == 0
        self.tn = _pick(D, (1024, 512, 256, 128))

    def mod_row(self, i):
        return jnp.minimum((i * self.tm) // self.L, self.B)

    def tiles(self, with_ctx):
        return (self.Rv if with_ctx else self.RL) // self.tm


def _ada_body(c_ref, w_ref, b_ref, o_ref):
    c = c_ref[...]
    s = (c * jax.nn.sigmoid(c)).astype(BF16)
    o_ref[...] = jnp.dot(s, w_ref[...].astype(BF16), preferred_element_type=F32) + b_ref[...]


def ada_modulation(c_all, ada_w, ada_b):
    depth, D, N = ada_w.shape
    tn = _pick(N, (1024, 512, 256, 128))
    return pl.pallas_call(
        _ada_body,
        out_shape=jax.ShapeDtypeStruct((depth, SUBLANES, N), F32),
        grid=(depth, N // tn),
        in_specs=[pl.BlockSpec((SUBLANES, D), lambda l, j: (0, 0)),
                  pl.BlockSpec((None, D, tn), lambda l, j: (l, 0, j)),
                  pl.BlockSpec((None, 1, tn), lambda l, j: (l, 0, j))],
        out_specs=pl.BlockSpec((None, SUBLANES, tn), lambda l, j: (l, 0, j)),
        compiler_params=_cparams("parallel", "parallel"),
        name="ada_modulation",
    )(c_all, ada_w, ada_b.reshape(depth, 1, N))


def _mm_body(*refs, n_lhs, n_w, n_epi, prologue, epilogue):
    lhs = refs[:n_lhs]
    ws = refs[n_lhs:n_lhs + n_w]
    epi = refs[n_lhs + n_w:n_lhs + n_w + n_epi]
    outs = refs[n_lhs + n_w + n_epi:-1]
    hb = refs[-1]

    @pl.when(pl.program_id(1) == 0)
    def _():
        hb[...] = prologue(*lhs).astype(hb.dtype)

    h = hb[...]
    accs = [jnp.dot(h, w[...], preferred_element_type=F32) for w in ws]
    for o, v in zip(outs, epilogue(accs, *epi)):
        o[...] = v.astype(o.dtype).reshape(o.shape)


def fused_matmul(name, geo, n_tiles, lhs, w, n_split, epi, prologue, epilogue, outs, alias=None,
                 out_layout=None):
    K = w.shape[0]
    N = w.shape[1] // n_split
    tm, tn = geo.tm, geo.tn
    nj = N // tn
    w_specs = [pl.BlockSpec((K, tn), functools.partial(lambda i, j, s: (0, s * nj + j), s=s))
               for s in range(n_split)]
    arrays = [a for a, _ in lhs] + [w] * n_split + [a for a, _ in epi]
    specs = [s for _, s in lhs] + w_specs + [s for _, s in epi]
    body = functools.partial(_mm_body, n_lhs=len(lhs), n_w=n_split, n_epi=len(epi),
                             prologue=prologue, epilogue=epilogue)
    out_shape, out_spec = out_layout or ((geo.R, N), pl.BlockSpec((tm, tn), lambda i, j: (i, j)))
    res = pl.pallas_call(
        body,
        out_shape=[jax.ShapeDtypeStruct(out_shape, dt) for dt in outs],
        grid=(n_tiles, nj),
        in_specs=specs,
        out_specs=[out_spec for _ in outs],
        scratch_shapes=[pltpu.VMEM((tm, K), BF16)],
        input_output_aliases=alias or {},
        compiler_params=_cparams("parallel", "arbitrary"),
        name=name,
    )(*arrays)
    return res


def _row_full(geo, K):
    return pl.BlockSpec((geo.tm, K), lambda i, j: (i, 0))


def _row_tile(geo):
    return pl.BlockSpec((geo.tm, geo.tn), lambda i, j: (i, j))


def _vec_full(K):
    return pl.BlockSpec((1, K), lambda i, j: (0, 0))


def _mod_full(geo, layer, k):
    return pl.BlockSpec((None, None, None, 1, geo.D),
                        lambda i, j: (layer, geo.mod_row(i), k, 0, 0))


def _mod_tile(geo, layer, k):
    return pl.BlockSpec((None, None, None, 1, geo.tn),
                        lambda i, j: (layer, geo.mod_row(i), k, 0, j))


def _norm_mod(x_ref, g_ref, sh_ref, sc_ref):
    x = x_ref[...]
    y = x * lax.rsqrt(jnp.mean(x * x, axis=-1, keepdims=True) + EPS)
    return (y * g_ref[...]) * (1.0 + sc_ref[...]) + sh_ref[...]


def _norm_mod_inputs(geo, xc, gain, modp, layer, k_shift):
    return [(xc, _row_full(geo, geo.D)), (gain.reshape(1, geo.D), _vec_full(geo.D)),
            (modp, _mod_full(geo, layer, k_shift)), (modp, _mod_full(geo, layer, k_shift + 1))]


def _residual_epilogue(accs, x_ref, gate_ref):
    return [x_ref[...] + gate_ref[...] * accs[0]]


def _conv_in_epilogue(accs):
    gate_b, gate_c, v = accs
    return [gate_b, gate_c * v]


def _conv_prologue(gb_ref, z_ref, zp_ref, zn_ref, wc_ref, *, geo):
    tm = geo.tm
    z = z_ref[...].astype(F32)
    t = lax.broadcasted_iota(jnp.int32, (tm, 1), 0)
    row = pl.program_id(0) * tm + t
    is_lat = row < geo.RL
    pos = jnp.where(is_lat, row % geo.L, (row - geo.RL) % geo.Lc)
    first = pos == 0
    last = pos == jnp.where(is_lat, geo.L - 1, geo.Lc - 1)
    halo_last = zp_ref[CONV_HALO - 1:CONV_HALO, :].astype(F32)
    z_prev = jnp.where(t == 0, halo_last, pltpu.roll(z, 1, axis=0))
    z_prev = jnp.where(first, 0.0, z_prev)
    z_next = jnp.where(t == tm - 1, zn_ref[0:1, :].astype(F32), pltpu.roll(z, tm - 1, axis=0))
    z_next = jnp.where(last, 0.0, z_next)
    wc = wc_ref[...]
    conv = wc[0:1, :] * z_prev + wc[1:2, :] * z + wc[2:3, :] * z_next
    return gb_ref[...].astype(F32) * conv


def conv_mixer(geo, xc, modp, layer, gain, w_in, w_conv, w_out, with_ctx):
    D, tm = geo.D, geo.tm
    n_tiles = geo.tiles(with_ctx)
    gb, z = fused_matmul("conv_in", geo, n_tiles,
                         _norm_mod_inputs(geo, xc, gain, modp, layer, 0),
                         w_in.astype(BF16), 3, [], _norm_mod, _conv_in_epilogue, [BF16, BF16])
    r8 = tm // CONV_HALO
    last_blk = geo.R // CONV_HALO - 1
    halo_prev = pl.BlockSpec((CONV_HALO, D), lambda i, j: (jnp.maximum(i * r8 - 1, 0), 0))
    halo_next = pl.BlockSpec((CONV_HALO, D), lambda i, j: (jnp.minimum((i + 1) * r8, last_blk), 0))
    lhs = [(gb, _row_full(geo, D)), (z, _row_full(geo, D)), (z, halo_prev), (z, halo_next),
           (w_conv, pl.BlockSpec((3, D), lambda i, j: (0, 0)))]
    epi = [(xc, _row_tile(geo)), (modp, _mod_tile(geo, layer, 2))]
    (out,) = fused_matmul("conv_out", geo, n_tiles, lhs, w_out.astype(BF16), 1, epi,
                          functools.partial(_conv_prologue, geo=geo), _residual_epilogue, [F32],
                          alias={len(lhs) + 1: 0})
    return out


def _s5_operators(a_re, a_im, log_dt, b_re, b_im, c_re, c_im, T):
    G, P = a_re.shape[1:]
    Cg = b_re.shape[-1]
    k = jnp.arange(T + 1, dtype=F32)[:, None, None]
    mt = 0.0
    bst, cst, aT = [], [], []
    lag = jnp.arange(T)[:, None] - jnp.arange(T)[None, :]
    for d in range(2):
        dt = jnp.exp(log_dt[d])[:, None]
        mag = jnp.exp(k * dt * a_re[d])
        pw_r, pw_i = mag * jnp.cos(k * dt * a_im[d]), mag * jnp.sin(k * dt * a_im[d])
        den = a_re[d] * a_re[d] + a_im[d] * a_im[d]
        nr = (pw_r[1] - 1.0) * a_re[d] + pw_i[1] * a_im[d]
        ni = pw_i[1] * a_re[d] - (pw_r[1] - 1.0) * a_im[d]
        fr, fi = (nr / den)[..., None], (ni / den)[..., None]
        bb_r = fr * b_re[d] - fi * b_im[d]
        bb_i = fr * b_im[d] + fi * b_re[d]
        w_r = c_re[d][None] * pw_r[:T, :, None, :] - c_im[d][None] * pw_i[:T, :, None, :]
        w_i = c_re[d][None] * pw_i[:T, :, None, :] + c_im[d][None] * pw_r[:T, :, None, :]
        kk = (jnp.einsum('kgjp,gpc->gkjc', w_r, bb_r, precision='highest')
              - jnp.einsum('kgjp,gpc->gkjc', w_i, bb_i, precision='highest'))
        dlag = lag if d == 0 else -lag
        m = jnp.where((dlag >= 0)[None, :, :, None, None], kk[:, jnp.clip(dlag, 0, T - 1)], 0.0)
        mt = mt + m.transpose(0, 2, 4, 1, 3).reshape(G, T * Cg, T * Cg)
        e = jnp.arange(T - 1, -1, -1) if d == 0 else jnp.arange(T)
        er, ei = pw_r[e], pw_i[e]
        s_r = er[:, :, :, None] * bb_r[None] - ei[:, :, :, None] * bb_i[None]
        s_i = er[:, :, :, None] * bb_i[None] + ei[:, :, :, None] * bb_r[None]
        bst.append(tuple(v.transpose(1, 0, 3, 2).reshape(G, T * Cg, P) for v in (s_r, s_i)))
        e = jnp.arange(1, T + 1) if d == 0 else jnp.arange(T, 0, -1)
        er, ei = pw_r[e], pw_i[e]
        o_r = c_re[d][None] * er[:, :, None, :] - c_im[d][None] * ei[:, :, None, :]
        o_i = -(c_re[d][None] * ei[:, :, None, :] + c_im[d][None] * er[:, :, None, :])
        cst.append(tuple(v.transpose(1, 3, 0, 2).reshape(G, P, T * Cg) for v in (o_r, o_i)))
        aT.append((pw_r[T], pw_i[T]))
    return mt, bst, cst, aT


def _lane_tile_operators(mt, bst, cst, T, gl):
    G, TC, _ = mt.shape
    Cg = TC // T
    P = bst[0][0].shape[-1]
    GB = G // gl

    def stack_u_rows(m):
        n = m.shape[-1]
        return m.reshape(GB, gl, T, Cg, n).transpose(0, 2, 1, 3, 4).reshape(GB, T * gl * Cg, n)

    a_in = stack_u_rows(jnp.concatenate([m for d in range(2) for m in bst[d]], axis=2))
    a_intra = stack_u_rows(mt)
    a_read = jnp.stack([m.reshape(GB, gl * P, TC) for d in range(2) for m in cst[d]], axis=1)
    return a_in.astype(BF16), a_intra.astype(BF16), a_read.reshape(GB, 4 * gl * P, TC).astype(BF16)


def _spread_block_diag(a_ref, w_ref, row0, *, gl, row_blk, col_blk):
    n_rows, k = a_ref.shape
    n_cols = k * gl
    kr = lax.broadcasted_iota(jnp.int32, (k, n_cols), 0)
    kc = lax.broadcasted_iota(jnp.int32, (k, n_cols), 1)
    rep = jnp.where((kr // col_blk == kc // (gl * col_blk)) & (kr % col_blk == kc % col_blk),
                    1.0, 0.0).astype(a_ref.dtype)
    col_group = (lax.broadcasted_iota(jnp.int32, (1, n_cols), 1) // col_blk) % gl
    step = 512 if n_rows % 512 == 0 else n_rows
    for r0 in range(0, n_rows, step):
        row_group = ((lax.broadcasted_iota(jnp.int32, (step, 1), 0) + r0) // row_blk) % gl
        full = jnp.dot(a_ref[r0:r0 + step, :], rep, preferred_element_type=F32)
        w_ref[row0 + r0:row0 + r0 + step, :] = jnp.where(row_group == col_group, full, 0.0).astype(w_ref.dtype)


def _s5_state_in_body(*refs, T, gl, Cg, P):
    u_refs, a_ref, ds_ref, w_ref = refs[:T], refs[T], refs[T + 1], refs[T + 2]

    @pl.when(pl.program_id(1) == 0)
    def _():
        _spread_block_diag(a_ref, w_ref, 0, gl=gl, row_blk=Cg, col_blk=P)

    u = jnp.concatenate([r[...] for r in u_refs], axis=1)
    res = jnp.dot(u, w_ref[...], preferred_element_type=F32)
    w = res.shape[1] // 4
    for k in range(4):
        ds_ref[k // 2, k % 2] = res[:, k * w:(k + 1) * w]


def _s5_scan_body(ds_ref, a_ref, s_ref, *, B, nc_lat, nc_ctx):
    a_re, a_im = a_ref[0], a_ref[1]
    zero = jnp.zeros((B, ds_ref.shape[-1]), F32)

    def segment(first, n, reverse, state):
        def step(i, st):
            s_re, s_im = st
            k = first + ((n - 1 - i) if reverse else i)
            for b in range(B):
                s_ref[0, pl.ds(k + b * n, 1), :] = s_re[b:b + 1].astype(s_ref.dtype)
                s_ref[1, pl.ds(k + b * n, 1), :] = s_im[b:b + 1].astype(s_ref.dtype)
            d_re = jnp.concatenate([ds_ref[0, pl.ds(k + b * n, 1), :] for b in range(B)], axis=0)
            d_im = jnp.concatenate([ds_ref[1, pl.ds(k + b * n, 1), :] for b in range(B)], axis=0)
            return a_re * s_re - a_im * s_im + d_re, a_re * s_im + a_im * s_re + d_im
        return lax.fori_loop(0, n, step, state)

    for d in range(2):
        @pl.when(pl.program_id(0) == d)
        def _():
            st = segment(B * nc_lat, nc_ctx, d == 1, (zero, zero))
            segment(0, nc_lat, d == 1, st)


def _s5_out_body(*refs, T, gl, Cg, P):
    u_refs = refs[:T]
    s_ref, ai_ref, ar_ref, d_ref, y_ref, w_ref = refs[T:]

    @pl.when(pl.program_id(1) == 0)
    def _():
        _spread_block_diag(ai_ref, w_ref, 0, gl=gl, row_blk=Cg, col_blk=Cg)
        _spread_block_diag(ar_ref, w_ref, ai_ref.shape[0], gl=gl, row_blk=P, col_blk=Cg)

    u = jnp.concatenate([r[...] for r in u_refs], axis=1)
    s = jnp.concatenate([s_ref[d, part] for d in range(2) for part in range(2)], axis=1).astype(u.dtype)
    y = (jnp.dot(jnp.concatenate([u, s], axis=1), w_ref[...], preferred_element_type=F32)
         + u.astype(F32) * d_ref[...])
    z = jax.nn.gelu(y).astype(y_ref.dtype)
    for t in range(T):
        y_ref[t] = z[:, t * LANES:(t + 1) * LANES]


def _glu_epilogue(accs, x_ref, gate_ref):
    val, gate = accs
    return [x_ref[...] + gate_ref[...] * (val * jax.nn.sigmoid(gate))]


def _identity_prologue(z_ref):
    return z_ref[...]


def _untranspose_prologue(z_ref):
    T, cb, D = z_ref.shape
    v = z_ref[...].reshape(T * cb, D)
    r = lax.broadcasted_iota(jnp.int32, (T * cb, T * cb), 0)
    c = lax.broadcasted_iota(jnp.int32, (T * cb, T * cb), 1)
    perm = jnp.where(c == (r % T) * cb + r // T, 1.0, 0.0).astype(v.dtype)
    return jnp.dot(perm, v, preferred_element_type=F32)


def s5_mixer(geo, xc, modp, layer, gain, w_in, a_re, a_im, log_dt, b_re, b_im, c_re, c_im, d_skip, w_glu):
    B, L, Lc, D = geo.B, geo.L, geo.Lc, geo.D
    G, P = a_re.shape[1:]
    Cg, T = D // G, SSM_CHUNK
    gl = LANES // Cg
    GB = G // gl
    nc_ctx, nc_lat = Lc // T, L // T
    rows = geo.Rv // T
    rt = _pick(rows, (272, 256, 128, 64, 32, 16))
    (u,) = fused_matmul("s5_in", geo, geo.tiles(True), _norm_mod_inputs(geo, xc, gain, modp, layer, 0),
                        w_in.astype(BF16), 1, [], _norm_mod, lambda accs: accs, [BF16])
    u2 = u.reshape(geo.R // T, T * D)
    u_specs = [pl.BlockSpec((rt, LANES), functools.partial(lambda gb, r, t: (r, t * (D // LANES) + gb), t=t))
               for t in range(T)]

    mt, bst, cst, aT = _s5_operators(a_re, a_im, log_dt, b_re, b_im, c_re, c_im, T)
    a_in, a_intra, a_read = _lane_tile_operators(mt, bst, cst, T, gl)
    a_tab = jnp.stack([jnp.stack([v.reshape(1, G * P) for v in aT[d]]) for d in range(2)])
    d_lane = jnp.tile(d_skip.reshape(GB, 1, LANES), (1, 1, T))

    KU, KS = T * LANES, 4 * gl * P
    dims = dict(T=T, gl=gl, Cg=Cg, P=P)
    ds = pl.pallas_call(
        functools.partial(_s5_state_in_body, **dims),
        out_shape=jax.ShapeDtypeStruct((2, 2, rows, G * P), F32),
        grid=(GB, rows // rt),
        in_specs=u_specs + [pl.BlockSpec((None, KU, 4 * P), lambda gb, r: (gb, 0, 0))],
        out_specs=pl.BlockSpec((2, 2, rt, gl * P), lambda gb, r: (0, 0, r, gb)),
        scratch_shapes=[pltpu.VMEM((KU, KS), BF16)],
        compiler_params=_cparams("parallel", "arbitrary"),
        name="s5_state_in",
    )(*([u2] * T), a_in)

    lb = _pick(G * P, (1024, 512, 256, 128))
    s_start = pl.pallas_call(
        functools.partial(_s5_scan_body, B=B, nc_lat=nc_lat, nc_ctx=nc_ctx),
        out_shape=jax.ShapeDtypeStruct((2, 2, rows, G * P), F32),
        grid=(2, G * P // lb),
        in_specs=[pl.BlockSpec((None, 2, rows, lb), lambda d, j: (d, 0, 0, j)),
                  pl.BlockSpec((None, 2, 1, lb), lambda d, j: (d, 0, 0, j))],
        out_specs=pl.BlockSpec((None, 2, rows, lb), lambda d, j: (d, 0, 0, j)),
        compiler_params=_cparams("parallel", "parallel"),
        name="s5_scan",
    )(ds, a_tab)

    y3 = pl.pallas_call(
        functools.partial(_s5_out_body, **dims),
        out_shape=jax.ShapeDtypeStruct((T, rows, D), BF16),
        grid=(GB, rows // rt),
        in_specs=u_specs + [pl.BlockSpec((2, 2, rt, gl * P), lambda gb, r: (0, 0, r, gb)),
                            pl.BlockSpec((None, KU, T * Cg), lambda gb, r: (gb, 0, 0)),
                            pl.BlockSpec((None, KS, T * Cg), lambda gb, r: (gb, 0, 0)),
                            pl.BlockSpec((None, 1, KU), lambda gb, r: (gb, 0, 0))],
        out_specs=pl.BlockSpec((T, rt, LANES), lambda gb, r: (0, r, gb)),
        scratch_shapes=[pltpu.VMEM((KU + KS, KU), BF16)],
        compiler_params=_cparams("parallel", "arbitrary"),
        name="s5_out",
    )(*([u2] * T), s_start, a_intra, a_read, d_lane)

    g2 = Geo(B, L, Lc, D, tm=T * 16)
    lhs = [(y3, pl.BlockSpec((T, g2.tm // T, D), lambda i, j: (0, i, 0)))]
    epi = [(xc, _row_tile(g2)), (modp, _mod_tile(g2, layer, 2))]
    (out,) = fused_matmul("s5_glu", g2, g2.tiles(True), lhs, w_glu.astype(BF16), 2, epi,
                          _untranspose_prologue, _glu_epilogue, [F32], alias={len(lhs) + 2: 0})
    return out


def _rope_tables(geo, head_dim):
    n_freq = head_dim // 4
    pos = jnp.arange(geo.L)
    inv_freq = ROPE_THETA ** (-jnp.arange(n_freq, dtype=F32) / n_freq)
    d = jnp.arange(LANES) % head_dim
    axis, half, f = d // (2 * n_freq), (d % (2 * n_freq)) // n_freq, d % n_freq
    p = jnp.where(axis[None, :] == 0, (pos // GRID_W)[:, None], (pos % GRID_W)[:, None]).astype(F32)
    ang = p * inv_freq[f][None, :]
    cos = jnp.cos(ang)
    sin = jnp.sin(ang) * jnp.where(half == 0, -1.0, 1.0)[None, :]
    rest = geo.R - geo.RL
    return (jnp.concatenate([jnp.tile(cos, (geo.B, 1)), jnp.ones((rest, LANES), F32)], axis=0),
            jnp.concatenate([jnp.tile(sin, (geo.B, 1)), jnp.zeros((rest, LANES), F32)], axis=0))


def _qkv_epilogue(accs, cos_ref, sin_ref, *, scale, n_freq):
    q, k, v = accs
    reps = q.shape[1] // LANES
    cos = jnp.concatenate([cos_ref[...]] * reps, axis=1)
    sin = jnp.concatenate([sin_ref[...]] * reps, axis=1)
    tn = q.shape[1]
    lane = lax.broadcasted_iota(jnp.int32, (1, tn), 1)
    first_half = (lane % (2 * n_freq)) < n_freq

    def rope(x):
        partner = jnp.where(first_half, pltpu.roll(x, tn - n_freq, axis=1), pltpu.roll(x, n_freq, axis=1))
        return x * cos + partner * sin

    return [rope(q) * scale, rope(k), v]


def _attn_body(q_ref, kl_ref, kc_ref, vl_ref, vc_ref, lam_ref, g_ref, o_ref, k_all, v_aug, *, lambda_init):
    q = q_ref[...]
    hw = q.shape[1]
    head_dim = hw // 2
    L = kl_ref.shape[0]
    lane = lax.broadcasted_iota(jnp.int32, (1, hw), 1)
    nt = (((1,), (1,)), ((), ()))

    @pl.when(pl.program_id(2) == 0)
    def _():
        k_all[:L, :] = kl_ref[...]
        k_all[L:, :] = kc_ref[...]
        v_aug[:L, :hw] = vl_ref[...]
        v_aug[L:, :hw] = vc_ref[...]
        v_aug[:, hw:] = jnp.ones((v_aug.shape[0], hw), v_aug.dtype)

    def attend(qm):
        s = lax.dot_general(qm, k_all[...], nt, preferred_element_type=F32)
        p = jnp.exp2(s - jnp.max(s, axis=1, keepdims=True)).astype(BF16)
        o = jnp.dot(p, v_aug[...], preferred_element_type=F32)
        return o[:, :hw] / o[:, hw:hw + 1]

    lp = lam_ref[...]
    lam = (jnp.exp(jnp.sum(lp[0:1] * lp[1:2], axis=1, keepdims=True))
           - jnp.exp(jnp.sum(lp[2:3] * lp[3:4], axis=1, keepdims=True)) + lambda_init)
    rows = q.shape[0] // ATTN_SUBTILES
    for r in range(ATTN_SUBTILES):
        qr = q[r * rows:(r + 1) * rows]
        o = (attend(jnp.where(lane < head_dim, qr, jnp.zeros_like(qr)))
             - lam * attend(jnp.where(lane >= head_dim, qr, jnp.zeros_like(qr))))
        o = o * lax.rsqrt(jnp.mean(o * o, axis=-1, keepdims=True) + EPS) * g_ref[...]
        o_ref[r * rows:(r + 1) * rows, :] = (o * (1.0 - lambda_init)).astype(o_ref.dtype)


def attn_mixer(geo, xc, modp, layer, gain, w_qkv, lam_params, subln_g, w_o):
    B, L, Lc, D = geo.B, geo.L, geo.Lc, geo.D
    head_dim = lam_params.shape[1]
    hw = 2 * head_dim
    assert hw == LANES and geo.R % Lc == 0
    H = D // hw
    lambda_init = 0.8 - 0.6 * math.exp(-0.3 * layer)
    cos, sin = _rope_tables(geo, head_dim)
    tab = pl.BlockSpec((geo.tm, LANES), lambda i, j: (i, 0))
    q, k, v = fused_matmul(
        "attn_qkv", geo, geo.tiles(True), _norm_mod_inputs(geo, xc, gain, modp, layer, 0),
        w_qkv.astype(BF16), 3, [(cos, tab), (sin, tab)], _norm_mod,
        functools.partial(_qkv_epilogue, scale=head_dim ** -0.5 * math.log2(math.e),
                          n_freq=head_dim // 4), [BF16] * 3)
    tq = _pick(L, (2048, 1024, 512, 256, 128))
    nq = L // tq
    ctx_blk = geo.RL // Lc
    lat_kv = pl.BlockSpec((L, hw), lambda b, h, i: (b, h))
    ctx_kv = pl.BlockSpec((Lc, hw), lambda b, h, i: (ctx_blk + b, h))
    o = pl.pallas_call(
        functools.partial(_attn_body, lambda_init=lambda_init),
        out_shape=jax.ShapeDtypeStruct((geo.R, D), BF16),
        grid=(B, H, nq),
        in_specs=[pl.BlockSpec((tq, hw), lambda b, h, i: (b * nq + i, h)),
                  lat_kv, ctx_kv, lat_kv, ctx_kv,
                  pl.BlockSpec((4, head_dim), lambda b, h, i: (0, 0)),
                  pl.BlockSpec((1, hw), lambda b, h, i: (0, 0))],
        out_specs=pl.BlockSpec((tq, hw), lambda b, h, i: (b * nq + i, h)),
        scratch_shapes=[pltpu.VMEM((L + Lc, hw), BF16), pltpu.VMEM((L + Lc, 2 * hw), BF16)],
        compiler_params=_cparams("parallel", "parallel", "arbitrary"),
        name="diff_attention",
    )(q, k, k, v, v, lam_params, subln_g.reshape(1, hw))
    lhs = [(o, _row_full(geo, D))]
    epi = [(xc, _row_tile(geo)), (modp, _mod_tile(geo, layer, 2))]
    (out,) = fused_matmul("attn_out", geo, geo.tiles(False), lhs, w_o.astype(BF16), 1, epi,
                          _identity_prologue, _residual_epilogue, [F32], alias={len(lhs) + 1: 0})
    return out


def _router_body(x_ref, g_ref, sh_ref, sc_ref, rt_ref, hf_ref, aff_ref):
    h = _norm_mod(x_ref, g_ref, sh_ref, sc_ref)
    half = h.shape[1] // 2
    hb = h.astype(BF16)
    hbf = hb.astype(F32)
    hi = pltpu.bitcast(hbf[:, :half], jnp.uint32)
    lo = lax.shift_right_logical(pltpu.bitcast(hbf[:, half:], jnp.uint32), jnp.uint32(16))
    hf_ref[...] = hi | lo
    h_lo = (h - hbf).astype(BF16)
    rt = rt_ref[...]
    r_hi = rt.astype(BF16)
    r_lo = (rt - r_hi.astype(F32)).astype(BF16)
    nt = (((1,), (1,)), ((), ()))
    logits = (lax.dot_general(r_hi, hb, nt, preferred_element_type=F32)
              + lax.dot_general(r_hi, h_lo, nt, preferred_element_type=F32)
              + lax.dot_general(r_lo, hb, nt, preferred_element_type=F32))
    m = jnp.max(logits, axis=0, keepdims=True)
    e = jnp.exp(logits - m)
    aff_ref[...] = e / jnp.sum(e, axis=0, keepdims=True)


def moe_router(geo, xc, modp, layer, gain, router, with_ctx):
    D, tm = geo.D, geo.tm
    E = router.shape[1]
    i_only = lambda f: (lambda i: f(i, 0))
    specs = [s for _, s in _norm_mod_inputs(geo, xc, gain, modp, layer, 3)]
    specs = [pl.BlockSpec(s.block_shape, i_only(s.index_map)) for s in specs]
    return pl.pallas_call(
        _router_body,
        out_shape=[jax.ShapeDtypeStruct((geo.R, D // 2), jnp.uint32),
                   jax.ShapeDtypeStruct((E, geo.R), F32)],
        grid=(geo.tiles(with_ctx),),
        in_specs=specs + [pl.BlockSpec((E, D), lambda i: (0, 0))],
        out_specs=[pl.BlockSpec((tm, D // 2), lambda i: (i, 0)),
                   pl.BlockSpec((E, tm), lambda i: (0, i))],
        compiler_params=_cparams("parallel"),
        name="moe_router",
    )(xc, gain.reshape(1, D), modp, modp, router.T)


def _prefix_count(x, tri):
    n = x.shape[1]
    off = jnp.zeros((x.shape[0], 1), F32)
    parts = []
    for j in range(n // LANES):
        blk = x[:, j * LANES:(j + 1) * LANES]
        inc = jnp.dot(blk.astype(BF16), tri, preferred_element_type=F32)
        parts.append(inc - blk + off)
        off = off + inc[:, LANES - 1:LANES]
    return jnp.concatenate(parts, axis=1)


def _select_body(aff_ref, idx_ref, g_ref, *, cap):
    a = aff_ref[...]
    E, n = a.shape
    bits = pltpu.bitcast(a, jnp.int32)
    thr = jnp.zeros((E, 1), jnp.int32)
    for k in range(30, -1, -1):
        cand = thr | jnp.int32(1 << k)
        cnt = jnp.sum((bits >= cand).astype(F32), axis=1, keepdims=True)
        thr = jnp.where(cnt >= cap, cand, thr)
    gt = bits > thr
    eq = bits == thr
    need = cap - jnp.sum(gt.astype(F32), axis=1, keepdims=True)
    r = lax.broadcasted_iota(jnp.int32, (LANES, LANES), 0)
    c = lax.broadcasted_iota(jnp.int32, (LANES, LANES), 1)
    tri = (r <= c).astype(BF16)
    sel = gt | (eq & (_prefix_count(eq.astype(F32), tri) < need))
    self32 = sel.astype(F32)
    slot = _prefix_count(self32, tri)
    slot = jnp.where(sel, slot, -1.0)
    t = lax.broadcasted_iota(jnp.int32, (1, n), 1)
    t_hi = (t // 64).astype(F32)
    t_lo = (t % 64).astype(F32)
    a1 = a.astype(BF16).astype(F32)
    a2 = (a - a1).astype(BF16).astype(F32)
    a3 = (a - a1 - a2).astype(BF16).astype(F32)
    s_iota = lax.broadcasted_iota(jnp.int32, (cap, 1), 0).astype(F32)
    row = lax.broadcasted_iota(jnp.int32, (SUBLANES, 1), 0)
    nt = (((1,), (1,)), ((), ()))
    for e in range(E):
        onehot = jnp.where(slot[e:e + 1, :] == s_iota, 1.0, 0.0).astype(BF16)
        lhs = jnp.where(row == 0, t_hi, jnp.where(row == 1, t_lo, jnp.where(
            row == 2, a1[e:e + 1, :], jnp.where(row == 3, a2[e:e + 1, :], jnp.where(
                row == 4, a3[e:e + 1, :], 0.0))))).astype(BF16)
        res = lax.dot_general(lhs, onehot, nt, preferred_element_type=F32)
        idx_ref[e:e + 1, :] = (res[0:1, :] * 64.0 + res[1:2, :]).astype(jnp.int32)
        g_ref[e:e + 1, :] = res[2:3, :] + res[3:4, :] + res[4:5, :]


def moe_select(aff, n_samples, n, col_off):
    E = aff.shape[0]
    cap = max(1, CAPACITY_FACTOR * n // E)
    blk0 = col_off // n
    return pl.pallas_call(
        functools.partial(_select_body, cap=cap),
        out_shape=[jax.ShapeDtypeStruct((n_samples, E, cap), jnp.int32),
                   jax.ShapeDtypeStruct((n_samples, E, cap), F32)],
        grid=(n_samples,),
        in_specs=[pl.BlockSpec((E, n), lambda s: (0, blk0 + s))],
        out_specs=[pl.BlockSpec((None, E, cap), lambda s: (s, 0, 0)),
                   pl.BlockSpec((None, E, cap), lambda s: (s, 0, 0))],
        compiler_params=_cparams("parallel"),
        name="moe_select",
    )(aff)


def _gather_body(idx_ref, hf_ref, *rest, cap, n_exp):
    o_ref = rest[-1]
    base = (pl.program_id(0) * n_exp + pl.program_id(1)) * cap

    def step(i, carry):
        i0 = pl.multiple_of(i * SUBLANES, SUBLANES)
        rows = [hf_ref[pl.ds(idx_ref[base + i0 + k], 1), :] for k in range(SUBLANES)]
        for k in range(SUBLANES):
            o_ref[pl.ds(i0 + k, 1), :] = rows[k]
        return carry

    lax.fori_loop(0, cap // SUBLANES, step, 0)


def moe_gather(idx, hf, xg, rows_alloc, n, row_off, slot_off):
    S, E, cap = idx.shape
    W = hf.shape[1]
    rb, sb = row_off // n, slot_off // cap
    prev = [] if xg is None else [xg]
    grid_spec = pltpu.PrefetchScalarGridSpec(
        num_scalar_prefetch=1,
        grid=(S, E),
        in_specs=[pl.BlockSpec((n, W), lambda s, e, idx: (rb + s, 0))]
        + [pl.BlockSpec(memory_space=pl.ANY) for _ in prev],
        out_specs=pl.BlockSpec((None, cap, W), lambda s, e, idx: (e, sb + s, 0)),
    )
    return pl.pallas_call(
        functools.partial(_gather_body, cap=cap, n_exp=E),
        out_shape=jax.ShapeDtypeStruct((E, rows_alloc, W), hf.dtype),
        grid_spec=grid_spec,
        input_output_aliases={2: 0} if prev else {},
        compiler_params=_cparams("parallel", "arbitrary"),
        name="moe_gather",
    )(idx.reshape(-1), hf, *prev)


def _expert_body(xg_ref, g_ref, wg_ref, wu_ref, wd_ref, y_ref, wg_s, wu_s, wd_s, *, n_exp):
    j, k = pl.program_id(0), pl.program_id(1)

    @pl.when(j < n_exp)
    def _():
        dk, fk = wg_ref.shape[0], wd_ref.shape[0]
        rows_d = pl.ds(pl.multiple_of(k * dk, dk), dk)
        wg_s[j % 2, rows_d, :] = wg_ref[...].astype(BF16)
        wu_s[j % 2, rows_d, :] = wu_ref[...].astype(BF16)
        wd_s[j % 2, pl.ds(pl.multiple_of(k * fk, fk), fk), :] = wd_ref[...].astype(BF16)

    @pl.when(j == 0)
    def _():
        y_ref[...] = jnp.zeros_like(y_ref)

    @pl.when(j > 0)
    def _():
        cur = (j + 1) % 2
        u = xg_ref[...]
        half = u.shape[1]
        a = pltpu.bitcast(u & jnp.uint32(0xFFFF0000), F32).astype(BF16)
        b = pltpu.bitcast(lax.shift_left(u, jnp.uint32(16)), F32).astype(BF16)
        gate = (jnp.dot(a, wg_s[cur, :half, :], preferred_element_type=F32)
                + jnp.dot(b, wg_s[cur, half:, :], preferred_element_type=F32))
        up = (jnp.dot(a, wu_s[cur, :half, :], preferred_element_type=F32)
              + jnp.dot(b, wu_s[cur, half:, :], preferred_element_type=F32))
        act = (gate * jax.nn.sigmoid(gate) * up).astype(BF16)
        y_ref[...] = jnp.dot(act, wd_s[cur], preferred_element_type=F32) * g_ref[...]


def moe_experts(xg, g_col, rows, layer, w_gate, w_up, w_down):
    E, rows_alloc, W = xg.shape
    D, F = w_gate.shape[2], w_gate.shape[3]
    tr = max(t for t in range(16, 513, 16) if rows % t == 0)
    K = rows // tr
    assert D % (16 * K) == 0 and F % (16 * K) == 0
    w_exp = lambda j: jnp.minimum(j, E - 1)
    x_exp = lambda j: jnp.maximum(j - 1, 0)
    y_exp = lambda j: jnp.where(j == 0, E, j - 1)
    return pl.pallas_call(
        functools.partial(_expert_body, n_exp=E),
        out_shape=jax.ShapeDtypeStruct((E + 1, rows_alloc, D), F32),
        grid=(E + 1, K),
        in_specs=[pl.BlockSpec((None, tr, W), lambda j, k: (x_exp(j), k, 0)),
                  pl.BlockSpec((None, tr, 1), lambda j, k: (x_exp(j), k, 0)),
                  pl.BlockSpec((None, None, D // K, F), lambda j, k: (layer, w_exp(j), k, 0)),
                  pl.BlockSpec((None, None, D // K, F), lambda j, k: (layer, w_exp(j), k, 0)),
                  pl.BlockSpec((None, None, F // K, D), lambda j, k: (layer, w_exp(j), k, 0))],
        out_specs=pl.BlockSpec((None, tr, D), lambda j, k: (y_exp(j), k, 0)),
        scratch_shapes=[pltpu.VMEM((2, D, F), BF16), pltpu.VMEM((2, D, F), BF16),
                        pltpu.VMEM((2, F, D), BF16)],
        compiler_params=_cparams("arbitrary", "arbitrary"),
        name="moe_experts",
    )(xg, g_col, w_gate, w_up, w_down)


def _combine_body(idx_ref, y_ref, x_ref, gate_ref, o_ref, *, cap, n_exp):
    e = pl.program_id(2)
    base = (pl.program_id(0) * n_exp + e) * cap

    @pl.when(e == 0)
    def _():
        o_ref[...] = jnp.zeros_like(o_ref)

    def step(i, carry):
        i0 = pl.multiple_of(i * SUBLANES, SUBLANES)
        y8 = y_ref[pl.ds(i0, SUBLANES), :]
        tok = [idx_ref[base + i0 + k] for k in range(SUBLANES)]
        acc = [o_ref[pl.ds(tok[k], 1), :] for k in range(SUBLANES)]
        for k in range(SUBLANES):
            o_ref[pl.ds(tok[k], 1), :] = acc[k] + y8[k:k + 1, :]
        return carry

    lax.fori_loop(0, cap // SUBLANES, step, 0)

    @pl.when(e == n_exp - 1)
    def _():
        o_ref[...] = x_ref[...] + gate_ref[...] * o_ref[...]


def moe_combine(geo, idx, y, xc, modp, layer, n, row_off, slot_off, mod_ctx):
    S, E, cap = idx.shape
    D = geo.D
    dc = _pick(D, (1024, 512, 256, 128))
    rb, sb = row_off // n, slot_off // cap
    mod_row = (lambda s: geo.B) if mod_ctx else (lambda s: s)
    once = pl.Buffered(1)
    grid_spec = pltpu.PrefetchScalarGridSpec(
        num_scalar_prefetch=1,
        grid=(S, D // dc, E),
        in_specs=[pl.BlockSpec((None, cap, dc), lambda s, c, e, idx: (e, sb + s, c)),
                  pl.BlockSpec((n, dc), lambda s, c, e, idx: (rb + s, c), pipeline_mode=once),
                  pl.BlockSpec((None, None, None, 1, dc),
                               lambda s, c, e, idx: (layer, mod_row(s), 5, 0, c))],
        out_specs=pl.BlockSpec((n, dc), lambda s, c, e, idx: (rb + s, c), pipeline_mode=once),
    )
    return pl.pallas_call(
        functools.partial(_combine_body, cap=cap, n_exp=E),
        out_shape=jax.ShapeDtypeStruct(xc.shape, xc.dtype),
        grid_spec=grid_spec,
        input_output_aliases={2: 0},
        compiler_params=_cparams("parallel", "parallel", "arbitrary"),
        name="moe_combine",
    )(idx.reshape(-1), y, xc, modp)


def moe_layer(geo, xc, modp, layer, gain, router, w_gate, w_up, w_down, with_ctx):
    B, L, Lc = geo.B, geo.L, geo.Lc
    E = router.shape[1]
    hf, aff = moe_router(geo, xc, modp, layer, gain, router, with_ctx)
    groups = [(L, 0)] + ([(Lc, geo.RL)] if with_ctx else [])
    sel = [moe_select(aff, B, n, off) for n, off in groups]
    slot_offs = [0, B * sel[0][0].shape[2]]
    rows = sum(B * s[0].shape[2] for s in sel)
    cap_l = sel[0][0].shape[2]
    rows_alloc = -(-rows // cap_l) * cap_l
    xg = None
    for (n, off), (idx, _), so in zip(groups, sel, slot_offs):
        xg = moe_gather(idx, hf, xg, rows_alloc, n, off, so)
    g_col = jnp.concatenate([g.transpose(1, 0, 2).reshape(E, -1) for _, g in sel]
                            + [jnp.zeros((E, rows_alloc - rows), F32)], axis=1)[:, :, None]
    y = moe_experts(xg, g_col, rows, layer, w_gate, w_up, w_down)
    for k, ((n, off), (idx, _), so) in enumerate(zip(groups, sel, slot_offs)):
        xc = moe_combine(geo, idx, y, xc, modp, layer, n, off, so, mod_ctx=k == 1)
    return xc


def _final_norm_body(x_ref, g_ref, o_ref):
    x = x_ref[...]
    o_ref[...] = x * lax.rsqrt(jnp.mean(x * x, axis=-1, keepdims=True) + EPS) * g_ref[...]


def final_norm(geo, xc, gain):
    D, tm = geo.D, geo.tm
    return pl.pallas_call(
        _final_norm_body,
        out_shape=jax.ShapeDtypeStruct((geo.RL, D), F32),
        grid=(geo.RL // tm,),
        in_specs=[pl.BlockSpec((tm, D), lambda i: (i, 0)), pl.BlockSpec((1, D), lambda i: (0, 0))],
        out_specs=pl.BlockSpec((tm, D), lambda i: (i, 0)),
        compiler_params=_cparams("parallel"),
        name="final_norm",
    )(xc, gain.reshape(1, D))


def kernel(x, c, ctx, c_ctx, ada_w, ada_b, norm_mix_g, norm_ffn_g, final_norm_g, conv_w_in, conv_w, conv_w_out, ssm_w_in, ssm_a_re, ssm_a_im, ssm_log_dt, ssm_b_re, ssm_b_im, ssm_c_re, ssm_c_im, ssm_d, ssm_w_glu, attn_w_qkv, attn_lambda, attn_subln_g, attn_w_o, moe_router, moe_w_gate, moe_w_up, moe_w_down):
    B, L, D = x.shape
    Lc = ctx.shape[1]
    depth = ada_w.shape[0]
    geo = Geo(B, L, Lc, D)
    xc = jnp.concatenate([x.reshape(B * L, D), ctx.reshape(B * Lc, D),
                          jnp.zeros((geo.R - geo.Rv, D), F32)], axis=0)
    c_all = jnp.zeros((SUBLANES, D), F32).at[:B].set(c).at[B].set(c_ctx)
    modp = ada_modulation(c_all, ada_w, ada_b).reshape(depth, SUBLANES, 6, 1, D)
    ctx_live = [any(MIXER_READS_CTX[j % N_MIXERS] for j in range(i + 1, depth)) for i in range(depth)]
    for i in range(depth):
        kind, slot = i % N_MIXERS, i // N_MIXERS
        ctx_out = ctx_live[i]
        if kind == 0:
            xc = conv_mixer(geo, xc, modp, i, norm_mix_g[i], conv_w_in[slot], conv_w[slot],
                            conv_w_out[slot], ctx_out)
        elif kind == 1:
            xc = s5_mixer(geo, xc, modp, i, norm_mix_g[i], ssm_w_in[slot], ssm_a_re[slot], ssm_a_im[slot],
                          ssm_log_dt[slot], ssm_b_re[slot], ssm_b_im[slot], ssm_c_re[slot], ssm_c_im[slot],
                          ssm_d[slot], ssm_w_glu[slot])
        else:
            xc = attn_mixer(geo, xc, modp, i, norm_mix_g[i], attn_w_qkv[slot], attn_lambda[slot],
                            attn_subln_g[slot], attn_w_o[slot])
        xc = moe_layer(geo, xc, modp, i, norm_ffn_g[i], moe_router[i], moe_w_gate, moe_w_up, moe_w_down,
                       ctx_out)
    return final_norm(geo, xc, final_norm_g).reshape(B, L, D)
```

```python
import functools
import math

import jax
import jax.numpy as jnp
from jax import lax
from jax.experimental import pallas as pl
from jax.experimental.pallas import tpu as pltpu

F32 = jnp.float32
BF16 = jnp.bfloat16
EPS = 1e-6
N_MIXERS = 3
MIXER_READS_CTX = (False, True, True)
GRID_W = 64
ROPE_THETA = 10000.0
CAPACITY_FACTOR = 2
SSM_CHUNK = 16
SUBLANES = 8
CONV_HALO = 16
ATTN_SUBTILES = 8
LANES = 128
VMEM_LIMIT = 56 * 1024 * 1024


def _cparams(*sem):
    return pltpu.CompilerParams(dimension_semantics=sem, vmem_limit_bytes=VMEM_LIMIT)


def _pick(n, cands):
    for c in cands:
        if n % c == 0:
            return c
    raise ValueError(f"no tile in {cands} divides {n}")


class Geo:
    def __init__(self, B, L, Lc, D, tm=None, tn=None):
        self.B, self.L, self.Lc, self.D = B, L, Lc, D
        self.RL = B * L
        self.Rv = B * (L + Lc)
        self.R = -(-self.Rv // L) * L
        self.tm = tm or _pick(math.gcd(L, B * Lc), (512, 256, 128, 64, 32, 16, 8))
        assert L % self.tm == 0 and (B * Lc) % self.tm == 0
        self.tn = tn or _pick(D, (1024, 512, 256, 128))

    def whole_width(self):
        return Geo(self.B, self.L, self.Lc, self.D, tm=self.tm, tn=self.D)

    def mod_row(self, i):
        return jnp.minimum((i * self.tm) // self.L, self.B)

    def tiles(self, with_ctx):
        return (self.Rv if with_ctx else self.RL) // self.tm


def _ada_body(c_ref, w_ref, b_ref, o_ref):
    c = c_ref[...]
    s = (c * jax.nn.sigmoid(c)).astype(BF16)
    o_ref[...] = jnp.dot(s, w_ref[...].astype(BF16), preferred_element_type=F32) + b_ref[...]


def ada_modulation(c_all, ada_w, ada_b):
    depth, D, N = ada_w.shape
    tn = _pick(N, (1024, 512, 256, 128))
    return pl.pallas_call(
        _ada_body,
        out_shape=jax.ShapeDtypeStruct((depth, SUBLANES, N), F32),
        grid=(depth, N // tn),
        in_specs=[pl.BlockSpec((SUBLANES, D), lambda l, j: (0, 0)),
                  pl.BlockSpec((None, D, tn), lambda l, j: (l, 0, j)),
                  pl.BlockSpec((None, 1, tn), lambda l, j: (l, 0, j))],
        out_specs=pl.BlockSpec((None, SUBLANES, tn), lambda l, j: (l, 0, j)),
        compiler_params=_cparams("parallel", "parallel"),
        name="ada_modulation",
    )(c_all, ada_w, ada_b.reshape(depth, 1, N))


def _mm_body(*refs, n_lhs, n_w, n_epi, prologue, epilogue):
    lhs = refs[:n_lhs]
    ws = refs[n_lhs:n_lhs + n_w]
    epi = refs[n_lhs + n_w:n_lhs + n_w + n_epi]
    outs = refs[n_lhs + n_w + n_epi:-1]
    hb = refs[-1]

    @pl.when(pl.program_id(1) == 0)
    def _():
        hb[...] = prologue(*lhs).astype(hb.dtype)

    h = hb[...]
    accs = [jnp.dot(h, w[...], preferred_element_type=F32) for w in ws]
    for o, v in zip(outs, epilogue(accs, *epi)):
        o[...] = v.astype(o.dtype).reshape(o.shape)


def fused_matmul(name, geo, n_tiles, lhs, w, n_split, epi, prologue, epilogue, outs, alias=None,
                 out_layout=None):
    K = w.shape[0]
    N = w.shape[1] // n_split
    tm, tn = geo.tm, geo.tn
    nj = N // tn
    w_specs = [pl.BlockSpec((K, tn), functools.partial(lambda i, j, s: (0, s * nj + j), s=s))
               for s in range(n_split)]
    arrays = [a for a, _ in lhs] + [w] * n_split + [a for a, _ in epi]
    specs = [s for _, s in lhs] + w_specs + [s for _, s in epi]
    body = functools.partial(_mm_body, n_lhs=len(lhs), n_w=n_split, n_epi=len(epi),
                             prologue=prologue, epilogue=epilogue)
    out_shape, out_spec = out_layout or ((geo.R, N), pl.BlockSpec((tm, tn), lambda i, j: (i, j)))
    res = pl.pallas_call(
        body,
        out_shape=[jax.ShapeDtypeStruct(out_shape, dt) for dt in outs],
        grid=(n_tiles, nj),
        in_specs=specs,
        out_specs=[out_spec for _ in outs],
        scratch_shapes=[pltpu.VMEM((tm, K), BF16)],
        input_output_aliases=alias or {},
        compiler_params=_cparams("parallel", "arbitrary"),
        name=name,
    )(*arrays)
    return res


def _row_full(geo, K):
    return pl.BlockSpec((geo.tm, K), lambda i, j: (i, 0))


def _row_tile(geo):
    return pl.BlockSpec((geo.tm, geo.tn), lambda i, j: (i, j))


def _vec_full(K):
    return pl.BlockSpec((1, K), lambda i, j: (0, 0))


def _mod_full(geo, layer, k):
    return pl.BlockSpec((None, None, None, 1, geo.D),
                        lambda i, j: (layer, geo.mod_row(i), k, 0, 0))


def _mod_tile(geo, layer, k):
    return pl.BlockSpec((None, None, None, 1, geo.tn),
                        lambda i, j: (layer, geo.mod_row(i), k, 0, j))


def _norm_mod(x_ref, g_ref, sh_ref, sc_ref):
    x = x_ref[...]
    y = x * lax.rsqrt(jnp.mean(x * x, axis=-1, keepdims=True) + EPS)
    return (y * g_ref[...]) * (1.0 + sc_ref[...]) + sh_ref[...]


def _norm_mod_inputs(geo, xc, gain, modp, layer, k_shift):
    return [(xc, _row_full(geo, geo.D)), (gain.reshape(1, geo.D), _vec_full(geo.D)),
            (modp, _mod_full(geo, layer, k_shift)), (modp, _mod_full(geo, layer, k_shift + 1))]


def _residual_epilogue(accs, x_ref, gate_ref):
    return [x_ref[...] + gate_ref[...] * accs[0]]


def _conv_in_epilogue(accs):
    gate_b, gate_c, v = accs
    return [gate_b, gate_c * v]


def _conv_prologue(gb_ref, z_ref, zp_ref, zn_ref, wc_ref, *, geo):
    tm = geo.tm
    z = z_ref[...].astype(F32)
    t = lax.broadcasted_iota(jnp.int32, (tm, 1), 0)
    row = pl.program_id(0) * tm + t
    is_lat = row < geo.RL
    pos = jnp.where(is_lat, row % geo.L, (row - geo.RL) % geo.Lc)
    first = pos == 0
    last = pos == jnp.where(is_lat, geo.L - 1, geo.Lc - 1)
    halo_last = zp_ref[CONV_HALO - 1:CONV_HALO, :].astype(F32)
    z_prev = jnp.where(t == 0, halo_last, pltpu.roll(z, 1, axis=0))
    z_prev = jnp.where(first, 0.0, z_prev)
    z_next = jnp.where(t == tm - 1, zn_ref[0:1, :].astype(F32), pltpu.roll(z, tm - 1, axis=0))
    z_next = jnp.where(last, 0.0, z_next)
    wc = wc_ref[...]
    conv = wc[0:1, :] * z_prev + wc[1:2, :] * z + wc[2:3, :] * z_next
    return gb_ref[...].astype(F32) * conv


def conv_mixer(geo, xc, modp, layer, gain, w_in, w_conv, w_out, with_ctx):
    D, tm = geo.D, geo.tm
    n_tiles = geo.tiles(with_ctx)
    gb, z = fused_matmul("conv_in", geo, n_tiles,
                         _norm_mod_inputs(geo, xc, gain, modp, layer, 0),
                         w_in.astype(BF16), 3, [], _norm_mod, _conv_in_epilogue, [BF16, BF16])
    r8 = tm // CONV_HALO
    last_blk = geo.R // CONV_HALO - 1
    halo_prev = pl.BlockSpec((CONV_HALO, D), lambda i, j: (jnp.maximum(i * r8 - 1, 0), 0))
    halo_next = pl.BlockSpec((CONV_HALO, D), lambda i, j: (jnp.minimum((i + 1) * r8, last_blk), 0))
    lhs = [(gb, _row_full(geo, D)), (z, _row_full(geo, D)), (z, halo_prev), (z, halo_next),
           (w_conv, pl.BlockSpec((3, D), lambda i, j: (0, 0)))]
    gw = geo.whole_width()
    epi = [(xc, _row_tile(gw)), (modp, _mod_tile(gw, layer, 2))]
    (out,) = fused_matmul("conv_out", gw, n_tiles, lhs, w_out.astype(BF16), 1, epi,
                          functools.partial(_conv_prologue, geo=geo), _residual_epilogue, [F32],
                          alias={len(lhs) + 1: 0})
    return out


def _s5_operators(a_re, a_im, log_dt, b_re, b_im, c_re, c_im, T):
    G, P = a_re.shape[1:]
    Cg = b_re.shape[-1]
    k = jnp.arange(T + 1, dtype=F32)[:, None, None]
    mt = 0.0
    bst, cst, aT = [], [], []
    for d in range(2):
        dt = jnp.exp(log_dt[d])[:, None]
        mag = jnp.exp(k * dt * a_re[d])
        pw_r, pw_i = mag * jnp.cos(k * dt * a_im[d]), mag * jnp.sin(k * dt * a_im[d])
        den = a_re[d] * a_re[d] + a_im[d] * a_im[d]
        nr = (pw_r[1] - 1.0) * a_re[d] + pw_i[1] * a_im[d]
        ni = pw_i[1] * a_re[d] - (pw_r[1] - 1.0) * a_im[d]
        fr, fi = (nr / den)[..., None], (ni / den)[..., None]
        bb_r = fr * b_re[d] - fi * b_im[d]
        bb_i = fr * b_im[d] + fi * b_re[d]
        w_r = c_re[d][None] * pw_r[:T, :, None, :] - c_im[d][None] * pw_i[:T, :, None, :]
        w_i = c_re[d][None] * pw_i[:T, :, None, :] + c_im[d][None] * pw_r[:T, :, None, :]
        kk = (jnp.einsum('kgjp,gpc->gckj', w_r, bb_r, precision='highest')
              - jnp.einsum('kgjp,gpc->gckj', w_i, bb_i, precision='highest'))
        if d == 0:
            kf = kk.reshape(G, Cg, T * Cg)
            m = [jnp.pad(kf[..., :(T - x) * Cg], ((0, 0), (0, 0), (x * Cg, 0))) for x in range(T)]
        else:
            kr = kk[:, :, ::-1, :].reshape(G, Cg, T * Cg)
            m = [jnp.pad(kr[..., (T - 1 - x) * Cg:], ((0, 0), (0, 0), (0, (T - 1 - x) * Cg))) for x in range(T)]
        mt = mt + jnp.stack(m, axis=1)
        e = jnp.arange(T - 1, -1, -1) if d == 0 else jnp.arange(T)
        er, ei = pw_r[e], pw_i[e]
        s_r = er[:, :, :, None] * bb_r[None] - ei[:, :, :, None] * bb_i[None]
        s_i = er[:, :, :, None] * bb_i[None] + ei[:, :, :, None] * bb_r[None]
        bst.append(tuple(v.transpose(1, 0, 3, 2).reshape(G, T * Cg, P) for v in (s_r, s_i)))
        e = jnp.arange(1, T + 1) if d == 0 else jnp.arange(T, 0, -1)
        er, ei = pw_r[e], pw_i[e]
        o_r = c_re[d][None] * er[:, :, None, :] - c_im[d][None] * ei[:, :, None, :]
        o_i = -(c_re[d][None] * ei[:, :, None, :] + c_im[d][None] * er[:, :, None, :])
        cst.append(tuple(v.transpose(1, 3, 0, 2).reshape(G, P, T * Cg) for v in (o_r, o_i)))
        aT.append((pw_r[T], pw_i[T]))
    return mt, bst, cst, aT


def _lane_tile_operators(mt, bst, cst, T, gl):
    G, _, Cg, TC = mt.shape
    P = bst[0][0].shape[-1]
    GB = G // gl

    def stack_u_rows(m):
        n = m.shape[-1]
        return m.reshape(GB, gl, T, Cg, n).transpose(0, 2, 1, 3, 4).reshape(GB, T * gl * Cg, n)

    a_in = stack_u_rows(jnp.concatenate([m for d in range(2) for m in bst[d]], axis=2))
    a_intra = stack_u_rows(mt.reshape(G, T * Cg, TC))
    a_read = jnp.stack([m.reshape(GB, gl * P, TC) for d in range(2) for m in cst[d]], axis=1)
    return a_in.astype(BF16), a_intra.astype(BF16), a_read.reshape(GB, 4 * gl * P, TC).astype(BF16)


def _spread_block_diag(a_ref, w_ref, row0, *, gl, row_blk, col_blk):
    n_rows, k = a_ref.shape
    n_cols = k * gl
    kr = lax.broadcasted_iota(jnp.int32, (k, n_cols), 0)
    kc = lax.broadcasted_iota(jnp.int32, (k, n_cols), 1)
    rep = jnp.where((kr // col_blk == kc // (gl * col_blk)) & (kr % col_blk == kc % col_blk),
                    1.0, 0.0).astype(a_ref.dtype)
    col_group = (lax.broadcasted_iota(jnp.int32, (1, n_cols), 1) // col_blk) % gl
    step = 512 if n_rows % 512 == 0 else n_rows
    for r0 in range(0, n_rows, step):
        row_group = ((lax.broadcasted_iota(jnp.int32, (step, 1), 0) + r0) // row_blk) % gl
        full = jnp.dot(a_ref[r0:r0 + step, :], rep, preferred_element_type=F32)
        w_ref[row0 + r0:row0 + r0 + step, :] = jnp.where(row_group == col_group, full, 0.0).astype(w_ref.dtype)


def _chunk_rows(u_ref, T):
    n = u_ref.shape[0] // T
    return jnp.concatenate([u_ref[pl.ds(t, n, stride=T), :] for t in range(T)], axis=1)


def _s5_state_in_body(u_ref, a_ref, ds_ref, w_ref, *, T, gl, Cg, P):
    @pl.when(pl.program_id(1) == 0)
    def _():
        _spread_block_diag(a_ref, w_ref, 0, gl=gl, row_blk=Cg, col_blk=P)

    u = _chunk_rows(u_ref, T).astype(BF16)
    res = jnp.dot(u, w_ref[...], preferred_element_type=F32)
    w = res.shape[1] // 4
    for k in range(4):
        ds_ref[k // 2, k % 2] = res[:, k * w:(k + 1) * w]


def _s5_scan_body(ds_ref, a_ref, s_ref, *, B, nc_lat, nc_ctx):
    a_re, a_im = a_ref[0], a_ref[1]
    zero = jnp.zeros((B, ds_ref.shape[-1]), F32)

    def segment(first, n, reverse, state):
        def step(i, st):
            s_re, s_im = st
            k = first + ((n - 1 - i) if reverse else i)
            for b in range(B):
                s_ref[0, pl.ds(k + b * n, 1), :] = s_re[b:b + 1].astype(s_ref.dtype)
                s_ref[1, pl.ds(k + b * n, 1), :] = s_im[b:b + 1].astype(s_ref.dtype)
            d_re = jnp.concatenate([ds_ref[0, pl.ds(k + b * n, 1), :] for b in range(B)], axis=0)
            d_im = jnp.concatenate([ds_ref[1, pl.ds(k + b * n, 1), :] for b in range(B)], axis=0)
            return a_re * s_re - a_im * s_im + d_re, a_re * s_im + a_im * s_re + d_im
        return lax.fori_loop(0, n, step, state)

    for d in range(2):
        @pl.when(pl.program_id(0) == d)
        def _():
            st = segment(B * nc_lat, nc_ctx, d == 1, (zero, zero))
            segment(0, nc_lat, d == 1, st)


def _s5_out_body(u_ref, s_ref, ai_ref, ar_ref, d_ref, y_ref, w_ref, *, T, gl, Cg, P):
    @pl.when(pl.program_id(1) == 0)
    def _():
        _spread_block_diag(ai_ref, w_ref, 0, gl=gl, row_blk=Cg, col_blk=Cg)
        _spread_block_diag(ar_ref, w_ref, ai_ref.shape[0], gl=gl, row_blk=P, col_blk=Cg)

    u = _chunk_rows(u_ref, T)
    s = jnp.concatenate([s_ref[d, part] for d in range(2) for part in range(2)], axis=1)
    y = (jnp.dot(jnp.concatenate([u, s], axis=1).astype(BF16), w_ref[...], preferred_element_type=F32)
         + u * d_ref[...])
    z = jax.nn.gelu(y).astype(y_ref.dtype)
    for t in range(T):
        y_ref[t] = z[:, t * LANES:(t + 1) * LANES]


def _glu_epilogue(accs, x_ref, gate_ref):
    val, gate = accs
    return [x_ref[...] + gate_ref[...] * (val * jax.nn.sigmoid(gate))]


def _identity_prologue(z_ref):
    return z_ref[...]


def _untranspose_prologue(z_ref):
    T, cb, D = z_ref.shape
    v = z_ref[...].reshape(T * cb, D)
    r = lax.broadcasted_iota(jnp.int32, (T * cb, T * cb), 0)
    c = lax.broadcasted_iota(jnp.int32, (T * cb, T * cb), 1)
    perm = jnp.where(c == (r % T) * cb + r // T, 1.0, 0.0).astype(v.dtype)
    return jnp.dot(perm, v, preferred_element_type=F32)


def s5_mixer(geo, xc, modp, layer, gain, w_in, a_re, a_im, log_dt, b_re, b_im, c_re, c_im, d_skip, w_glu):
    B, L, Lc, D = geo.B, geo.L, geo.Lc, geo.D
    G, P = a_re.shape[1:]
    Cg, T = D // G, SSM_CHUNK
    gl = LANES // Cg
    GB = G // gl
    nc_ctx, nc_lat = Lc // T, L // T
    rows = geo.Rv // T
    rt = _pick(rows, (272, 256, 128, 64, 32, 16))
    (u,) = fused_matmul("s5_in", geo.whole_width(), geo.tiles(True),
                        _norm_mod_inputs(geo, xc, gain, modp, layer, 0),
                        w_in.astype(BF16), 1, [], _norm_mod, lambda accs: accs, [F32])
    u_specs = [pl.BlockSpec((rt * T, LANES), lambda gb, r: (r, gb))]

    mt, bst, cst, aT = _s5_operators(a_re, a_im, log_dt, b_re, b_im, c_re, c_im, T)
    a_in, a_intra, a_read = _lane_tile_operators(mt, bst, cst, T, gl)
    a_tab = jnp.stack([jnp.stack([v.reshape(1, G * P) for v in aT[d]]) for d in range(2)])
    d_lane = jnp.tile(d_skip.reshape(GB, 1, LANES), (1, 1, T))

    KU, KS = T * LANES, 4 * gl * P
    dims = dict(T=T, gl=gl, Cg=Cg, P=P)
    ds = pl.pallas_call(
        functools.partial(_s5_state_in_body, **dims),
        out_shape=jax.ShapeDtypeStruct((2, 2, rows, G * P), F32),
        grid=(GB, rows // rt),
        in_specs=u_specs + [pl.BlockSpec((None, KU, 4 * P), lambda gb, r: (gb, 0, 0))],
        out_specs=pl.BlockSpec((2, 2, rt, gl * P), lambda gb, r: (0, 0, r, gb)),
        scratch_shapes=[pltpu.VMEM((KU, KS), BF16)],
        compiler_params=_cparams("parallel", "arbitrary"),
        name="s5_state_in",
    )(u, a_in)

    lb = _pick(G * P, (1024, 512, 256, 128))
    s_start = pl.pallas_call(
        functools.partial(_s5_scan_body, B=B, nc_lat=nc_lat, nc_ctx=nc_ctx),
        out_shape=jax.ShapeDtypeStruct((2, 2, rows, G * P), F32),
        grid=(2, G * P // lb),
        in_specs=[pl.BlockSpec((None, 2, rows, lb), lambda d, j: (d, 0, 0, j)),
                  pl.BlockSpec((None, 2, 1, lb), lambda d, j: (d, 0, 0, j))],
        out_specs=pl.BlockSpec((None, 2, rows, lb), lambda d, j: (d, 0, 0, j)),
        compiler_params=_cparams("parallel", "parallel"),
        name="s5_scan",
    )(ds, a_tab)

    y3 = pl.pallas_call(
        functools.partial(_s5_out_body, **dims),
        out_shape=jax.ShapeDtypeStruct((T, rows, D), BF16),
        grid=(GB, rows // rt),
        in_specs=u_specs + [pl.BlockSpec((2, 2, rt, gl * P), lambda gb, r: (0, 0, r, gb)),
                            pl.BlockSpec((None, KU, T * Cg), lambda gb, r: (gb, 0, 0)),
                            pl.BlockSpec((None, KS, T * Cg), lambda gb, r: (gb, 0, 0)),
                            pl.BlockSpec((None, 1, KU), lambda gb, r: (gb, 0, 0))],
        out_specs=pl.BlockSpec((T, rt, LANES), lambda gb, r: (0, r, gb)),
        scratch_shapes=[pltpu.VMEM((KU + KS, KU), BF16)],
        compiler_params=_cparams("parallel", "arbitrary"),
        name="s5_out",
    )(u, s_start, a_intra, a_read, d_lane)

    g2 = Geo(B, L, Lc, D, tm=T * 16)
    lhs = [(y3, pl.BlockSpec((T, g2.tm // T, D), lambda i, j: (0, i, 0)))]
    epi = [(xc, _row_tile(g2)), (modp, _mod_tile(g2, layer, 2))]
    (out,) = fused_matmul("s5_glu", g2, g2.tiles(True), lhs, w_glu.astype(BF16), 2, epi,
                          _untranspose_prologue, _glu_epilogue, [F32], alias={len(lhs) + 2: 0})
    return out


def _rope_tables(geo, head_dim):
    n_freq = head_dim // 4
    pos = jnp.arange(geo.L)
    inv_freq = ROPE_THETA ** (-jnp.arange(n_freq, dtype=F32) / n_freq)
    d = jnp.arange(LANES) % head_dim
    axis, half, f = d // (2 * n_freq), (d % (2 * n_freq)) // n_freq, d % n_freq
    p = jnp.where(axis[None, :] == 0, (pos // GRID_W)[:, None], (pos % GRID_W)[:, None]).astype(F32)
    ang = p * inv_freq[f][None, :]
    cos = jnp.cos(ang)
    sin = jnp.sin(ang) * jnp.where(half == 0, -1.0, 1.0)[None, :]
    rest = geo.R - geo.RL
    return (jnp.concatenate([jnp.tile(cos, (geo.B, 1)), jnp.ones((rest, LANES), F32)], axis=0),
            jnp.concatenate([jnp.tile(sin, (geo.B, 1)), jnp.zeros((rest, LANES), F32)], axis=0))


def _qkv_epilogue(accs, cos_ref, sin_ref, *, scale, n_freq):
    q, k, v = accs
    reps = q.shape[1] // LANES
    cos = jnp.concatenate([cos_ref[...]] * reps, axis=1)
    sin = jnp.concatenate([sin_ref[...]] * reps, axis=1)
    tn = q.shape[1]
    lane = lax.broadcasted_iota(jnp.int32, (1, tn), 1)
    first_half = (lane % (2 * n_freq)) < n_freq

    def rope(x):
        partner = jnp.where(first_half, pltpu.roll(x, tn - n_freq, axis=1), pltpu.roll(x, n_freq, axis=1))
        return x * cos + partner * sin

    return [rope(q) * scale, rope(k), v]


def _attn_body(q_ref, kl_ref, kc_ref, vl_ref, vc_ref, lam_ref, g_ref, o_ref, k_all, v_aug, *, lambda_init):
    q = q_ref[...]
    hw = q.shape[1]
    head_dim = hw // 2
    L = kl_ref.shape[0]
    lane = lax.broadcasted_iota(jnp.int32, (1, hw), 1)
    nt = (((1,), (1,)), ((), ()))

    @pl.when(pl.program_id(2) == 0)
    def _():
        k_all[:L, :] = kl_ref[...]
        k_all[L:, :] = kc_ref[...]
        v_aug[:L, :hw] = vl_ref[...]
        v_aug[L:, :hw] = vc_ref[...]
        v_aug[:, hw:] = jnp.ones((v_aug.shape[0], hw), v_aug.dtype)

    def attend(qm):
        s = lax.dot_general(qm, k_all[...], nt, preferred_element_type=F32)
        p = jnp.exp2(s - jnp.max(s, axis=1, keepdims=True)).astype(BF16)
        o = jnp.dot(p, v_aug[...], preferred_element_type=F32)
        return o[:, :hw] / o[:, hw:hw + 1]

    lp = lam_ref[...]
    lam = (jnp.exp(jnp.sum(lp[0:1] * lp[1:2], axis=1, keepdims=True))
           - jnp.exp(jnp.sum(lp[2:3] * lp[3:4], axis=1, keepdims=True)) + lambda_init)
    rows = q.shape[0] // ATTN_SUBTILES
    for r in range(ATTN_SUBTILES):
        qr = q[r * rows:(r + 1) * rows]
        o = (attend(jnp.where(lane < head_dim, qr, jnp.zeros_like(qr)))
             - lam * attend(jnp.where(lane >= head_dim, qr, jnp.zeros_like(qr))))
        o = o * lax.rsqrt(jnp.mean(o * o, axis=-1, keepdims=True) + EPS) * g_ref[...]
        o_ref[r * rows:(r + 1) * rows, :] = (o * (1.0 - lambda_init)).astype(o_ref.dtype)


def attn_mixer(geo, xc, modp, layer, gain, w_qkv, lam_params, subln_g, w_o):
    B, L, Lc, D = geo.B, geo.L, geo.Lc, geo.D
    head_dim = lam_params.shape[1]
    hw = 2 * head_dim
    assert hw == LANES and geo.R % Lc == 0
    H = D // hw
    lambda_init = 0.8 - 0.6 * math.exp(-0.3 * layer)
    cos, sin = _rope_tables(geo, head_dim)
    tab = pl.BlockSpec((geo.tm, LANES), lambda i, j: (i, 0))
    q, k, v = fused_matmul(
        "attn_qkv", geo, geo.tiles(True), _norm_mod_inputs(geo, xc, gain, modp, layer, 0),
        w_qkv.astype(BF16), 3, [(cos, tab), (sin, tab)], _norm_mod,
        functools.partial(_qkv_epilogue, scale=head_dim ** -0.5 * math.log2(math.e),
                          n_freq=head_dim // 4), [BF16] * 3)
    tq = _pick(L, (2048, 1024, 512, 256, 128))
    nq = L // tq
    ctx_blk = geo.RL // Lc
    lat_kv = pl.BlockSpec((L, hw), lambda b, h, i: (b, h))
    ctx_kv = pl.BlockSpec((Lc, hw), lambda b, h, i: (ctx_blk + b, h))
    o = pl.pallas_call(
        functools.partial(_attn_body, lambda_init=lambda_init),
        out_shape=jax.ShapeDtypeStruct((geo.R, D), BF16),
        grid=(B, H, nq),
        in_specs=[pl.BlockSpec((tq, hw), lambda b, h, i: (b * nq + i, h)),
                  lat_kv, ctx_kv, lat_kv, ctx_kv,
                  pl.BlockSpec((4, head_dim), lambda b, h, i: (0, 0)),
                  pl.BlockSpec((1, hw), lambda b, h, i: (0, 0))],
        out_specs=pl.BlockSpec((tq, hw), lambda b, h, i: (b * nq + i, h)),
        scratch_shapes=[pltpu.VMEM((L + Lc, hw), BF16), pltpu.VMEM((L + Lc, 2 * hw), BF16)],
        compiler_params=_cparams("parallel", "parallel", "arbitrary"),
        name="diff_attention",
    )(q, k, k, v, v, lam_params, subln_g.reshape(1, hw))
    lhs = [(o, _row_full(geo, D))]
    gw = geo.whole_width()
    epi = [(xc, _row_tile(gw)), (modp, _mod_tile(gw, layer, 2))]
    (out,) = fused_matmul("attn_out", gw, geo.tiles(False), lhs, w_o.astype(BF16), 1, epi,
                          _identity_prologue, _residual_epilogue, [F32], alias={len(lhs) + 1: 0})
    return out


def _router_body(x_ref, g_ref, sh_ref, sc_ref, rt_ref, hf_ref, aff_ref):
    h = _norm_mod(x_ref, g_ref, sh_ref, sc_ref)
    half = h.shape[1] // 2
    hb = h.astype(BF16)
    hbf = hb.astype(F32)
    hi = pltpu.bitcast(hbf[:, :half], jnp.uint32)
    lo = lax.shift_right_logical(pltpu.bitcast(hbf[:, half:], jnp.uint32), jnp.uint32(16))
    hf_ref[...] = hi | lo
    h_lo = (h - hbf).astype(BF16)
    rt = rt_ref[...]
    r_hi = rt.astype(BF16)
    r_lo = (rt - r_hi.astype(F32)).astype(BF16)
    nt = (((1,), (1,)), ((), ()))
    logits = (lax.dot_general(r_hi, hb, nt, preferred_element_type=F32)
              + lax.dot_general(r_hi, h_lo, nt, preferred_element_type=F32)
              + lax.dot_general(r_lo, hb, nt, preferred_element_type=F32))
    m = jnp.max(logits, axis=0, keepdims=True)
    e = jnp.exp(logits - m)
    aff_ref[...] = e / jnp.sum(e, axis=0, keepdims=True)


def moe_router(geo, xc, modp, layer, gain, router, with_ctx):
    D, tm = geo.D, geo.tm
    E = router.shape[1]
    i_only = lambda f: (lambda i: f(i, 0))
    specs = [s for _, s in _norm_mod_inputs(geo, xc, gain, modp, layer, 3)]
    specs = [pl.BlockSpec(s.block_shape, i_only(s.index_map)) for s in specs]
    return pl.pallas_call(
        _router_body,
        out_shape=[jax.ShapeDtypeStruct((geo.R, D // 2), jnp.uint32),
                   jax.ShapeDtypeStruct((E, geo.R), F32)],
        grid=(geo.tiles(with_ctx),),
        in_specs=specs + [pl.BlockSpec((E, D), lambda i: (0, 0))],
        out_specs=[pl.BlockSpec((tm, D // 2), lambda i: (i, 0)),
                   pl.BlockSpec((E, tm), lambda i: (0, i))],
        compiler_params=_cparams("parallel"),
        name="moe_router",
    )(xc, gain.reshape(1, D), modp, modp, router.T)


def _prefix_count(x, tri):
    n = x.shape[1]
    off = jnp.zeros((x.shape[0], 1), F32)
    parts = []
    for j in range(n // LANES):
        blk = x[:, j * LANES:(j + 1) * LANES]
        inc = jnp.dot(blk.astype(BF16), tri, preferred_element_type=F32)
        parts.append(inc - blk + off)
        off = off + inc[:, LANES - 1:LANES]
    return jnp.concatenate(parts, axis=1)


def _select_body(aff_ref, idx_ref, g_ref, *, cap):
    a = aff_ref[...]
    E, n = a.shape
    bits = pltpu.bitcast(a, jnp.int32)
    thr = jnp.zeros((E, 1), jnp.int32)
    for k in range(30, -1, -1):
        cand = thr | jnp.int32(1 << k)
        cnt = jnp.sum((bits >= cand).astype(F32), axis=1, keepdims=True)
        thr = jnp.where(cnt >= cap, cand, thr)
    gt = bits > thr
    eq = bits == thr
    need = cap - jnp.sum(gt.astype(F32), axis=1, keepdims=True)
    r = lax.broadcasted_iota(jnp.int32, (LANES, LANES), 0)
    c = lax.broadcasted_iota(jnp.int32, (LANES, LANES), 1)
    tri = (r <= c).astype(BF16)
    sel = gt | (eq & (_prefix_count(eq.astype(F32), tri) < need))
    self32 = sel.astype(F32)
    slot = _prefix_count(self32, tri)
    slot = jnp.where(sel, slot, -1.0)
    t = lax.broadcasted_iota(jnp.int32, (1, n), 1)
    t_hi = (t // 64).astype(F32)
    t_lo = (t % 64).astype(F32)
    a1 = a.astype(BF16).astype(F32)
    a2 = (a - a1).astype(BF16).astype(F32)
    a3 = (a - a1 - a2).astype(BF16).astype(F32)
    s_iota = lax.broadcasted_iota(jnp.int32, (cap, 1), 0).astype(F32)
    row = lax.broadcasted_iota(jnp.int32, (SUBLANES, 1), 0)
    nt = (((1,), (1,)), ((), ()))
    for e in range(E):
        onehot = jnp.where(slot[e:e + 1, :] == s_iota, 1.0, 0.0).astype(BF16)
        lhs = jnp.where(row == 0, t_hi, jnp.where(row == 1, t_lo, jnp.where(
            row == 2, a1[e:e + 1, :], jnp.where(row == 3, a2[e:e + 1, :], jnp.where(
                row == 4, a3[e:e + 1, :], 0.0))))).astype(BF16)
        res = lax.dot_general(lhs, onehot, nt, preferred_element_type=F32)
        idx_ref[e:e + 1, :] = (res[0:1, :] * 64.0 + res[1:2, :]).astype(jnp.int32)
        g_ref[e:e + 1, :] = res[2:3, :] + res[3:4, :] + res[4:5, :]


def moe_select(aff, n_samples, n, col_off):
    E = aff.shape[0]
    cap = max(1, CAPACITY_FACTOR * n // E)
    blk0 = col_off // n
    return pl.pallas_call(
        functools.partial(_select_body, cap=cap),
        out_shape=[jax.ShapeDtypeStruct((n_samples, E, cap), jnp.int32),
                   jax.ShapeDtypeStruct((n_samples, E, cap), F32)],
        grid=(n_samples,),
        in_specs=[pl.BlockSpec((E, n), lambda s: (0, blk0 + s))],
        out_specs=[pl.BlockSpec((None, E, cap), lambda s: (s, 0, 0)),
                   pl.BlockSpec((None, E, cap), lambda s: (s, 0, 0))],
        compiler_params=_cparams("parallel"),
        name="moe_select",
    )(aff)


def _gather_body(idx_ref, hf_ref, *rest, cap, n_exp):
    o_ref = rest[-1]
    base = (pl.program_id(0) * n_exp + pl.program_id(1)) * cap

    def step(i, carry):
        i0 = pl.multiple_of(i * SUBLANES, SUBLANES)
        rows = [hf_ref[pl.ds(idx_ref[base + i0 + k], 1), :] for k in range(SUBLANES)]
        for k in range(SUBLANES):
            o_ref[pl.ds(i0 + k, 1), :] = rows[k]
        return carry

    lax.fori_loop(0, cap // SUBLANES, step, 0)


def moe_gather(idx, hf, xg, rows_alloc, n, row_off, slot_off):
    S, E, cap = idx.shape
    W = hf.shape[1]
    rb, sb = row_off // n, slot_off // cap
    prev = [] if xg is None else [xg]
    grid_spec = pltpu.PrefetchScalarGridSpec(
        num_scalar_prefetch=1,
        grid=(S, E),
        in_specs=[pl.BlockSpec((n, W), lambda s, e, idx: (rb + s, 0))]
        + [pl.BlockSpec(memory_space=pl.ANY) for _ in prev],
        out_specs=pl.BlockSpec((None, cap, W), lambda s, e, idx: (e, sb + s, 0)),
    )
    return pl.pallas_call(
        functools.partial(_gather_body, cap=cap, n_exp=E),
        out_shape=jax.ShapeDtypeStruct((E, rows_alloc, W), hf.dtype),
        grid_spec=grid_spec,
        input_output_aliases={2: 0} if prev else {},
        compiler_params=_cparams("parallel", "arbitrary"),
        name="moe_gather",
    )(idx.reshape(-1), hf, *prev)


def _expert_body(xg_ref, g_ref, wg_ref, wu_ref, wd_ref, y_ref, wg_s, wu_s, wd_s, *, n_exp):
    j, k = pl.program_id(0), pl.program_id(1)

    @pl.when(j < n_exp)
    def _():
        dk, fk = wg_ref.shape[0], wd_ref.shape[0]
        rows_d = pl.ds(pl.multiple_of(k * dk, dk), dk)
        wg_s[j % 2, rows_d, :] = wg_ref[...].astype(BF16)
        wu_s[j % 2, rows_d, :] = wu_ref[...].astype(BF16)
        wd_s[j % 2, pl.ds(pl.multiple_of(k * fk, fk), fk), :] = wd_ref[...].astype(BF16)

    @pl.when(j == 0)
    def _():
        y_ref[...] = jnp.zeros_like(y_ref)

    @pl.when(j > 0)
    def _():
        cur = (j + 1) % 2
        u = xg_ref[...]
        half = u.shape[1]
        a = pltpu.bitcast(u & jnp.uint32(0xFFFF0000), F32).astype(BF16)
        b = pltpu.bitcast(lax.shift_left(u, jnp.uint32(16)), F32).astype(BF16)
        gate = (jnp.dot(a, wg_s[cur, :half, :], preferred_element_type=F32)
                + jnp.dot(b, wg_s[cur, half:, :], preferred_element_type=F32))
        up = (jnp.dot(a, wu_s[cur, :half, :], preferred_element_type=F32)
              + jnp.dot(b, wu_s[cur, half:, :], preferred_element_type=F32))
        act = (gate * jax.nn.sigmoid(gate) * up).astype(BF16)
        y_ref[...] = jnp.dot(act, wd_s[cur], preferred_element_type=F32) * g_ref[...]


def moe_experts(xg, g_col, rows, layer, w_gate, w_up, w_down):
    E, rows_alloc, W = xg.shape
    D, F = w_gate.shape[2], w_gate.shape[3]
    tr = max(t for t in range(16, 513, 16) if rows % t == 0)
    K = rows // tr
    assert D % (16 * K) == 0 and F % (16 * K) == 0
    w_exp = lambda j: jnp.minimum(j, E - 1)
    x_exp = lambda j: jnp.maximum(j - 1, 0)
    y_exp = lambda j: jnp.where(j == 0, E, j - 1)
    return pl.pallas_call(
        functools.partial(_expert_body, n_exp=E),
        out_shape=jax.ShapeDtypeStruct((E + 1, rows_alloc, D), F32),
        grid=(E + 1, K),
        in_specs=[pl.BlockSpec((None, tr, W), lambda j, k: (x_exp(j), k, 0)),
                  pl.BlockSpec((None, tr, 1), lambda j, k: (x_exp(j), k, 0)),
                  pl.BlockSpec((None, None, D // K, F), lambda j, k: (layer, w_exp(j), k, 0)),
                  pl.BlockSpec((None, None, D // K, F), lambda j, k: (layer, w_exp(j), k, 0)),
                  pl.BlockSpec((None, None, F // K, D), lambda j, k: (layer, w_exp(j), k, 0))],
        out_specs=pl.BlockSpec((None, tr, D), lambda j, k: (y_exp(j), k, 0)),
        scratch_shapes=[pltpu.VMEM((2, D, F), BF16), pltpu.VMEM((2, D, F), BF16),
                        pltpu.VMEM((2, F, D), BF16)],
        compiler_params=_cparams("arbitrary", "arbitrary"),
        name="moe_experts",
    )(xg, g_col, w_gate, w_up, w_down)


def _combine_body(idx_ref, y_ref, x_ref, gate_ref, o_ref, *, cap, n_exp):
    e = pl.program_id(2)
    base = (pl.program_id(0) * n_exp + e) * cap

    @pl.when(e == 0)
    def _():
        o_ref[...] = jnp.zeros_like(o_ref)

    def step(i, carry):
        i0 = pl.multiple_of(i * SUBLANES, SUBLANES)
        y8 = y_ref[pl.ds(i0, SUBLANES), :]
        tok = [idx_ref[base + i0 + k] for k in range(SUBLANES)]
        acc = [o_ref[pl.ds(tok[k], 1), :] for k in range(SUBLANES)]
        for k in range(SUBLANES):
            o_ref[pl.ds(tok[k], 1), :] = acc[k] + y8[k:k + 1, :]
        return carry

    lax.fori_loop(0, cap // SUBLANES, step, 0)

    @pl.when(e == n_exp - 1)
    def _():
        o_ref[...] = x_ref[...] + gate_ref[...] * o_ref[...]


def moe_combine(geo, idx, y, xc, modp, layer, n, row_off, slot_off, mod_ctx):
    S, E, cap = idx.shape
    D = geo.D
    dc = _pick(D, (1024, 512, 256, 128))
    rb, sb = row_off // n, slot_off // cap
    mod_row = (lambda s: geo.B) if mod_ctx else (lambda s: s)
    once = pl.Buffered(1)
    grid_spec = pltpu.PrefetchScalarGridSpec(
        num_scalar_prefetch=1,
        grid=(S, D // dc, E),
        in_specs=[pl.BlockSpec((None, cap, dc), lambda s, c, e, idx: (e, sb + s, c)),
                  pl.BlockSpec((n, dc), lambda s, c, e, idx: (rb + s, c), pipeline_mode=once),
                  pl.BlockSpec((None, None, None, 1, dc),
                               lambda s, c, e, idx: (layer, mod_row(s), 5, 0, c))],
        out_specs=pl.BlockSpec((n, dc), lambda s, c, e, idx: (rb + s, c), pipeline_mode=once),
    )
    return pl.pallas_call(
        functools.partial(_combine_body, cap=cap, n_exp=E),
        out_shape=jax.ShapeDtypeStruct(xc.shape, xc.dtype),
        grid_spec=grid_spec,
        input_output_aliases={2: 0},
        compiler_params=_cparams("parallel", "parallel", "arbitrary"),
        name="moe_combine",
    )(idx.reshape(-1), y, xc, modp)


def moe_layer(geo, xc, modp, layer, gain, router, w_gate, w_up, w_down, with_ctx):
    B, L, Lc = geo.B, geo.L, geo.Lc
    E = router.shape[1]
    hf, aff = moe_router(geo, xc, modp, layer, gain, router, with_ctx)
    groups = [(L, 0)] + ([(Lc, geo.RL)] if with_ctx else [])
    sel = [moe_select(aff, B, n, off) for n, off in groups]
    slot_offs = [0, B * sel[0][0].shape[2]]
    rows = sum(B * s[0].shape[2] for s in sel)
    cap_l = sel[0][0].shape[2]
    rows_alloc = -(-rows // cap_l) * cap_l
    xg = None
    for (n, off), (idx, _), so in zip(groups, sel, slot_offs):
        xg = moe_gather(idx, hf, xg, rows_alloc, n, off, so)
    g_col = jnp.concatenate([g.transpose(1, 0, 2).reshape(E, -1) for _, g in sel]
                            + [jnp.zeros((E, rows_alloc - rows), F32)], axis=1)[:, :, None]
    y = moe_experts(xg, g_col, rows, layer, w_gate, w_up, w_down)
    for k, ((n, off), (idx, _), so) in enumerate(zip(groups, sel, slot_offs)):
        xc = moe_combine(geo, idx, y, xc, modp, layer, n, off, so, mod_ctx=k == 1)
    return xc


def _final_norm_body(x_ref, g_ref, o_ref):
    x = x_ref[...]
    o_ref[...] = x * lax.rsqrt(jnp.mean(x * x, axis=-1, keepdims=True) + EPS) * g_ref[...]


def final_norm(geo, xc, gain):
    D, tm = geo.D, geo.tm
    return pl.pallas_call(
        _final_norm_body,
        out_shape=jax.ShapeDtypeStruct((geo.RL, D), F32),
        grid=(geo.RL // tm,),
        in_specs=[pl.BlockSpec((tm, D), lambda i: (i, 0)), pl.BlockSpec((1, D), lambda i: (0, 0))],
        out_specs=pl.BlockSpec((tm, D), lambda i: (i, 0)),
        compiler_params=_cparams("parallel"),
        name="final_norm",
    )(xc, gain.reshape(1, D))


def kernel(x, c, ctx, c_ctx, ada_w, ada_b, norm_mix_g, norm_ffn_g, final_norm_g, conv_w_in, conv_w, conv_w_out, ssm_w_in, ssm_a_re, ssm_a_im, ssm_log_dt, ssm_b_re, ssm_b_im, ssm_c_re, ssm_c_im, ssm_d, ssm_w_glu, attn_w_qkv, attn_lambda, attn_subln_g, attn_w_o, moe_router, moe_w_gate, moe_w_up, moe_w_down):
    B, L, D = x.shape
    Lc = ctx.shape[1]
    depth = ada_w.shape[0]
    geo = Geo(B, L, Lc, D)
    xc = jnp.concatenate([x.reshape(B * L, D), ctx.reshape(B * Lc, D),
                          jnp.zeros((geo.R - geo.Rv, D), F32)], axis=0)
    c_all = jnp.zeros((SUBLANES, D), F32).at[:B].set(c).at[B].set(c_ctx)
    modp = ada_modulation(c_all, ada_w, ada_b).reshape(depth, SUBLANES, 6, 1, D)
    ctx_live = [any(MIXER_READS_CTX[j % N_MIXERS] for j in range(i + 1, depth)) for i in range(depth)]
    for i in range(depth):
        kind, slot = i % N_MIXERS, i // N_MIXERS
        ctx_out = ctx_live[i]
        if kind == 0:
            xc = conv_mixer(geo, xc, modp, i, norm_mix_g[i], conv_w_in[slot], conv_w[slot],
                            conv_w_out[slot], ctx_out)
        elif kind == 1:
            xc = s5_mixer(geo, xc, modp, i, norm_mix_g[i], ssm_w_in[slot], ssm_a_re[slot], ssm_a_im[slot],
                          ssm_log_dt[slot], ssm_b_re[slot], ssm_b_im[slot], ssm_c_re[slot], ssm_c_im[slot],
                          ssm_d[slot], ssm_w_glu[slot])
        else:
            xc = attn_mixer(geo, xc, modp, i, norm_mix_g[i], attn_w_qkv[slot], attn_lambda[slot],
                            attn_subln_g[slot], attn_w_o[slot])
        xc = moe_layer(geo, xc, modp, i, norm_ffn_g[i], moe_router[i], moe_w_gate, moe_w_up, moe_w_down,
                       ctx_out)
    return final_norm(geo, xc, final_norm_g).reshape(B, L, D)
```

```python
import functools
import math

import jax
import jax.numpy as jnp
from jax import lax
from jax.experimental import pallas as pl
from jax.experimental.pallas import tpu as pltpu

F32 = jnp.float32
BF16 = jnp.bfloat16
EPS = 1e-6
N_MIXERS = 3
MIXER_READS_CTX = (False, True, True)
GRID_W = 64
ROPE_THETA = 10000.0
CAPACITY_FACTOR = 2
SSM_CHUNK = 16
SUBLANES = 8
CONV_HALO = 16
ATTN_SUBTILES = 8
LANES = 128
VMEM_LIMIT = 56 * 1024 * 1024


def _cparams(*sem):
    return pltpu.CompilerParams(dimension_semantics=sem, vmem_limit_bytes=VMEM_LIMIT)


def _pick(n, cands):
    for c in cands:
        if n % c == 0:
            return c
    raise ValueError(f"no tile in {cands} divides {n}")


class Geo:
    def __init__(self, B, L, Lc, D, tm=None, tn=None):
        self.B, self.L, self.Lc, self.D = B, L, Lc, D
        self.RL = B * L
        self.Rv = B * (L + Lc)
        self.R = -(-self.Rv // L) * L
        self.tm = tm or _pick(math.gcd(L, B * Lc), (512, 256, 128, 64, 32, 16, 8))
        assert L % self.tm == 0 and (B * Lc) % self.tm == 0
        self.tn = tn or _pick(D, (1024, 512, 256, 128))

    def whole_width(self):
        return Geo(self.B, self.L, self.Lc, self.D, tm=self.tm, tn=self.D)

    def mod_row(self, i):
        return jnp.minimum((i * self.tm) // self.L, self.B)

    def tiles(self, with_ctx):
        return (self.Rv if with_ctx else self.RL) // self.tm


def _ada_body(c_ref, w_ref, b_ref, o_ref):
    c = c_ref[...]
    s = (c * jax.nn.sigmoid(c)).astype(BF16)
    o_ref[...] = jnp.dot(s, w_ref[...].astype(BF16), preferred_element_type=F32) + b_ref[...]


def ada_modulation(c_all, ada_w, ada_b):
    depth, D, N = ada_w.shape
    tn = _pick(N, (1024, 512, 256, 128))
    return pl.pallas_call(
        _ada_body,
        out_shape=jax.ShapeDtypeStruct((depth, SUBLANES, N), F32),
        grid=(depth, N // tn),
        in_specs=[pl.BlockSpec((SUBLANES, D), lambda l, j: (0, 0)),
                  pl.BlockSpec((None, D, tn), lambda l, j: (l, 0, j)),
                  pl.BlockSpec((None, 1, tn), lambda l, j: (l, 0, j))],
        out_specs=pl.BlockSpec((None, SUBLANES, tn), lambda l, j: (l, 0, j)),
        compiler_params=_cparams("parallel", "parallel"),
        name="ada_modulation",
    )(c_all, ada_w, ada_b.reshape(depth, 1, N))


def _mm_body(*refs, n_lhs, n_w, n_epi, prologue, epilogue):
    lhs = refs[:n_lhs]
    ws = refs[n_lhs:n_lhs + n_w]
    epi = refs[n_lhs + n_w:n_lhs + n_w + n_epi]
    outs = refs[n_lhs + n_w + n_epi:-1]
    hb = refs[-1]

    @pl.when(pl.program_id(1) == 0)
    def _():
        hb[...] = prologue(*lhs).astype(hb.dtype)

    h = hb[...]
    accs = [jnp.dot(h, w[...], preferred_element_type=F32) for w in ws]
    for o, v in zip(outs, epilogue(accs, *epi)):
        o[...] = v.astype(o.dtype).reshape(o.shape)


def fused_matmul(name, geo, n_tiles, lhs, w, n_split, epi, prologue, epilogue, outs, alias=None,
                 out_layout=None):
    K = w.shape[0]
    N = w.shape[1] // n_split
    tm, tn = geo.tm, geo.tn
    nj = N // tn
    w_specs = [pl.BlockSpec((K, tn), functools.partial(lambda i, j, s: (0, s * nj + j), s=s))
               for s in range(n_split)]
    arrays = [a for a, _ in lhs] + [w] * n_split + [a for a, _ in epi]
    specs = [s for _, s in lhs] + w_specs + [s for _, s in epi]
    body = functools.partial(_mm_body, n_lhs=len(lhs), n_w=n_split, n_epi=len(epi),
                             prologue=prologue, epilogue=epilogue)
    out_shape, out_spec = out_layout or ((geo.R, N), pl.BlockSpec((tm, tn), lambda i, j: (i, j)))
    res = pl.pallas_call(
        body,
        out_shape=[jax.ShapeDtypeStruct(out_shape, dt) for dt in outs],
        grid=(n_tiles, nj),
        in_specs=specs,
        out_specs=[out_spec for _ in outs],
        scratch_shapes=[pltpu.VMEM((tm, K), BF16)],
        input_output_aliases=alias or {},
        compiler_params=_cparams("parallel", "arbitrary"),
        name=name,
    )(*arrays)
    return res


def _row_full(geo, K):
    return pl.BlockSpec((geo.tm, K), lambda i, j: (i, 0))


def _row_tile(geo):
    return pl.BlockSpec((geo.tm, geo.tn), lambda i, j: (i, j))


def _vec_full(K):
    return pl.BlockSpec((1, K), lambda i, j: (0, 0))


def _mod_full(geo, layer, k):
    return pl.BlockSpec((None, None, None, 1, geo.D),
                        lambda i, j: (layer, geo.mod_row(i), k, 0, 0))


def _mod_tile(geo, layer, k):
    return pl.BlockSpec((None, None, None, 1, geo.tn),
                        lambda i, j: (layer, geo.mod_row(i), k, 0, j))


def _norm_mod(x_ref, g_ref, sh_ref, sc_ref):
    x = x_ref[...]
    y = x * lax.rsqrt(jnp.mean(x * x, axis=-1, keepdims=True) + EPS)
    return (y * g_ref[...]) * (1.0 + sc_ref[...]) + sh_ref[...]


def _norm_mod_inputs(geo, xc, gain, modp, layer, k_shift):
    return [(xc, _row_full(geo, geo.D)), (gain.reshape(1, geo.D), _vec_full(geo.D)),
            (modp, _mod_full(geo, layer, k_shift)), (modp, _mod_full(geo, layer, k_shift + 1))]


def _residual_epilogue(accs, x_ref, gate_ref):
    return [x_ref[...] + gate_ref[...] * accs[0]]


def _conv_in_epilogue(accs):
    gate_b, gate_c, v = accs
    return [gate_b, gate_c * v]


def _conv_prologue(gb_ref, z_ref, zp_ref, zn_ref, wc_ref, *, geo):
    tm = geo.tm
    z = z_ref[...].astype(F32)
    t = lax.broadcasted_iota(jnp.int32, (tm, 1), 0)
    row = pl.program_id(0) * tm + t
    is_lat = row < geo.RL
    pos = jnp.where(is_lat, row % geo.L, (row - geo.RL) % geo.Lc)
    first = pos == 0
    last = pos == jnp.where(is_lat, geo.L - 1, geo.Lc - 1)
    halo_last = zp_ref[CONV_HALO - 1:CONV_HALO, :].astype(F32)
    z_prev = jnp.where(t == 0, halo_last, pltpu.roll(z, 1, axis=0))
    z_prev = jnp.where(first, 0.0, z_prev)
    z_next = jnp.where(t == tm - 1, zn_ref[0:1, :].astype(F32), pltpu.roll(z, tm - 1, axis=0))
    z_next = jnp.where(last, 0.0, z_next)
    wc = wc_ref[...]
    conv = wc[0:1, :] * z_prev + wc[1:2, :] * z + wc[2:3, :] * z_next
    return gb_ref[...].astype(F32) * conv


def conv_mixer(geo, xc, modp, layer, gain, w_in, w_conv, w_out, with_ctx):
    D, tm = geo.D, geo.tm
    n_tiles = geo.tiles(with_ctx)
    gb, z = fused_matmul("conv_in", geo, n_tiles,
                         _norm_mod_inputs(geo, xc, gain, modp, layer, 0),
                         w_in.astype(BF16), 3, [], _norm_mod, _conv_in_epilogue, [BF16, BF16])
    r8 = tm // CONV_HALO
    last_blk = geo.R // CONV_HALO - 1
    halo_prev = pl.BlockSpec((CONV_HALO, D), lambda i, j: (jnp.maximum(i * r8 - 1, 0), 0))
    halo_next = pl.BlockSpec((CONV_HALO, D), lambda i, j: (jnp.minimum((i + 1) * r8, last_blk), 0))
    lhs = [(gb, _row_full(geo, D)), (z, _row_full(geo, D)), (z, halo_prev), (z, halo_next),
           (w_conv, pl.BlockSpec((3, D), lambda i, j: (0, 0)))]
    gw = geo.whole_width()
    epi = [(xc, _row_tile(gw)), (modp, _mod_tile(gw, layer, 2))]
    (out,) = fused_matmul("conv_out", gw, n_tiles, lhs, w_out.astype(BF16), 1, epi,
                          functools.partial(_conv_prologue, geo=geo), _residual_epilogue, [F32],
                          alias={len(lhs) + 1: 0})
    return out


def _s5_operators(a_re, a_im, log_dt, b_re, b_im, c_re, c_im, T):
    G, P = a_re.shape[1:]
    Cg = b_re.shape[-1]
    k = jnp.arange(T + 1, dtype=F32)[:, None, None]
    mt, bst, cst, aT = [], [], [], []
    for d in range(2):
        dt = jnp.exp(log_dt[d])[:, None]
        mag = jnp.exp(k * dt * a_re[d])
        pw_r, pw_i = mag * jnp.cos(k * dt * a_im[d]), mag * jnp.sin(k * dt * a_im[d])
        den = a_re[d] * a_re[d] + a_im[d] * a_im[d]
        nr = (pw_r[1] - 1.0) * a_re[d] + pw_i[1] * a_im[d]
        ni = pw_i[1] * a_re[d] - (pw_r[1] - 1.0) * a_im[d]
        fr, fi = (nr / den)[..., None], (ni / den)[..., None]
        bb_r = fr * b_re[d] - fi * b_im[d]
        bb_i = fr * b_im[d] + fi * b_re[d]
        w_r = c_re[d][None] * pw_r[:T, :, None, :] - c_im[d][None] * pw_i[:T, :, None, :]
        w_i = c_re[d][None] * pw_i[:T, :, None, :] + c_im[d][None] * pw_r[:T, :, None, :]
        kk = (jnp.einsum('kgjp,gpc->gckj', w_r, bb_r, precision='highest')
              - jnp.einsum('kgjp,gpc->gckj', w_i, bb_i, precision='highest'))
        mt.append((kk if d == 0 else kk[:, :, ::-1, :]).reshape(G * Cg, T * Cg))
        e = jnp.arange(T - 1, -1, -1) if d == 0 else jnp.arange(T)
        er, ei = pw_r[e], pw_i[e]
        s_r = er[:, :, :, None] * bb_r[None] - ei[:, :, :, None] * bb_i[None]
        s_i = er[:, :, :, None] * bb_i[None] + ei[:, :, :, None] * bb_r[None]
        bst.append(tuple(v.transpose(1, 0, 3, 2).reshape(G, T * Cg, P) for v in (s_r, s_i)))
        e = jnp.arange(1, T + 1) if d == 0 else jnp.arange(T, 0, -1)
        er, ei = pw_r[e], pw_i[e]
        o_r = c_re[d][None] * er[:, :, None, :] - c_im[d][None] * ei[:, :, None, :]
        o_i = -(c_re[d][None] * ei[:, :, None, :] + c_im[d][None] * er[:, :, None, :])
        cst.append(tuple(v.transpose(1, 3, 0, 2).reshape(G, P, T * Cg) for v in (o_r, o_i)))
        aT.append((pw_r[T], pw_i[T]))
    return mt, bst, cst, aT


def _lane_tile_operators(bst, cst, T, gl):
    G, TC, P = bst[0][0].shape
    Cg = TC // T
    GB = G // gl
    a_in = jnp.concatenate([m for d in range(2) for m in bst[d]], axis=2)
    a_in = a_in.reshape(GB, gl, T, Cg, 4 * P).transpose(0, 2, 1, 3, 4).reshape(GB, T * gl * Cg, 4 * P)
    a_read = jnp.stack([m.reshape(GB, gl * P, TC) for d in range(2) for m in cst[d]], axis=1)
    return a_in.astype(BF16), a_read.reshape(GB, 4 * gl * P, TC).astype(BF16)


def _toeplitz_rows(kf_ref, kr_ref, a_ref, *, T, Cg):
    n, TC = kf_ref.shape
    kf, kr = kf_ref[...], kr_ref[...]
    lane = lax.broadcasted_iota(jnp.int32, (1, TC), 1)
    for x in range(T):
        right, left = x * Cg, (T - 1 - x) * Cg
        fwd = jnp.where(lane >= right, kf if right == 0 else pltpu.roll(kf, right, axis=1), 0.0)
        bwd = jnp.where(lane < TC - left, kr if left == 0 else pltpu.roll(kr, TC - left, axis=1), 0.0)
        a_ref[x * n:(x + 1) * n, :] = (fwd + bwd).astype(a_ref.dtype)


def _spread_block_diag(a_ref, w_ref, row0, *, gl, row_blk, col_blk):
    n_rows, k = a_ref.shape
    n_cols = k * gl
    kr = lax.broadcasted_iota(jnp.int32, (k, n_cols), 0)
    kc = lax.broadcasted_iota(jnp.int32, (k, n_cols), 1)
    rep = jnp.where((kr // col_blk == kc // (gl * col_blk)) & (kr % col_blk == kc % col_blk),
                    1.0, 0.0).astype(a_ref.dtype)
    col_group = (lax.broadcasted_iota(jnp.int32, (1, n_cols), 1) // col_blk) % gl
    step = 512 if n_rows % 512 == 0 else n_rows
    for r0 in range(0, n_rows, step):
        row_group = ((lax.broadcasted_iota(jnp.int32, (step, 1), 0) + r0) // row_blk) % gl
        full = jnp.dot(a_ref[r0:r0 + step, :], rep, preferred_element_type=F32)
        w_ref[row0 + r0:row0 + r0 + step, :] = jnp.where(row_group == col_group, full, 0.0).astype(w_ref.dtype)


def _chunk_rows(u_ref, T):
    n = u_ref.shape[0] // T
    return jnp.concatenate([u_ref[pl.ds(t, n, stride=T), :] for t in range(T)], axis=1)


def _s5_state_in_body(u_ref, a_ref, ds_ref, w_ref, *, T, gl, Cg, P):
    @pl.when(pl.program_id(1) == 0)
    def _():
        _spread_block_diag(a_ref, w_ref, 0, gl=gl, row_blk=Cg, col_blk=P)

    u = _chunk_rows(u_ref, T).astype(BF16)
    res = jnp.dot(u, w_ref[...], preferred_element_type=F32)
    w = res.shape[1] // 4
    for k in range(4):
        ds_ref[k // 2, k % 2] = res[:, k * w:(k + 1) * w]


def _s5_scan_body(ds_ref, a_ref, s_ref, *, B, nc_lat, nc_ctx):
    a_re, a_im = a_ref[0], a_ref[1]
    zero = jnp.zeros((B, ds_ref.shape[-1]), F32)

    def segment(first, n, reverse, state):
        def step(i, st):
            s_re, s_im = st
            k = first + ((n - 1 - i) if reverse else i)
            for b in range(B):
                s_ref[0, pl.ds(k + b * n, 1), :] = s_re[b:b + 1].astype(s_ref.dtype)
                s_ref[1, pl.ds(k + b * n, 1), :] = s_im[b:b + 1].astype(s_ref.dtype)
            d_re = jnp.concatenate([ds_ref[0, pl.ds(k + b * n, 1), :] for b in range(B)], axis=0)
            d_im = jnp.concatenate([ds_ref[1, pl.ds(k + b * n, 1), :] for b in range(B)], axis=0)
            return a_re * s_re - a_im * s_im + d_re, a_re * s_im + a_im * s_re + d_im
        return lax.fori_loop(0, n, step, state)

    for d in range(2):
        @pl.when(pl.program_id(0) == d)
        def _():
            st = segment(B * nc_lat, nc_ctx, d == 1, (zero, zero))
            segment(0, nc_lat, d == 1, st)


def _s5_out_body(u_ref, s_ref, kf_ref, kr_ref, ar_ref, d_ref, y_ref, w_ref, ai_s, *, T, gl, Cg, P):
    @pl.when(pl.program_id(1) == 0)
    def _():
        _toeplitz_rows(kf_ref, kr_ref, ai_s, T=T, Cg=Cg)
        _spread_block_diag(ai_s, w_ref, 0, gl=gl, row_blk=Cg, col_blk=Cg)
        _spread_block_diag(ar_ref, w_ref, ai_s.shape[0], gl=gl, row_blk=P, col_blk=Cg)

    u = _chunk_rows(u_ref, T)
    s = jnp.concatenate([s_ref[d, part] for d in range(2) for part in range(2)], axis=1)
    y = (jnp.dot(jnp.concatenate([u, s], axis=1).astype(BF16), w_ref[...], preferred_element_type=F32)
         + u * d_ref[...])
    z = jax.nn.gelu(y)
    n = z.shape[0]
    for t in range(T):
        y_ref[pl.ds(t, n, stride=T), :] = z[:, t * LANES:(t + 1) * LANES]


def _glu_epilogue(accs, x_ref, gate_ref):
    val, gate = accs
    return [x_ref[...] + gate_ref[...] * (val * jax.nn.sigmoid(gate))]


def _identity_prologue(z_ref):
    return z_ref[...]


def s5_mixer(geo, xc, modp, layer, gain, w_in, a_re, a_im, log_dt, b_re, b_im, c_re, c_im, d_skip, w_glu):
    B, L, Lc, D = geo.B, geo.L, geo.Lc, geo.D
    G, P = a_re.shape[1:]
    Cg, T = D // G, SSM_CHUNK
    gl = LANES // Cg
    GB = G // gl
    nc_ctx, nc_lat = Lc // T, L // T
    rows = geo.Rv // T
    rt = _pick(rows, (272, 256, 128, 64, 32, 16))
    (u,) = fused_matmul("s5_in", geo.whole_width(), geo.tiles(True),
                        _norm_mod_inputs(geo, xc, gain, modp, layer, 0),
                        w_in.astype(BF16), 1, [], _norm_mod, lambda accs: accs, [F32])
    u_specs = [pl.BlockSpec((rt * T, LANES), lambda gb, r: (r, gb))]

    kk, bst, cst, aT = _s5_operators(a_re, a_im, log_dt, b_re, b_im, c_re, c_im, T)
    a_in, a_read = _lane_tile_operators(bst, cst, T, gl)
    kf, kr = (m.reshape(GB, gl * Cg, T * Cg) for m in kk)
    a_tab = jnp.stack([jnp.stack([v.reshape(1, G * P) for v in aT[d]]) for d in range(2)])
    d_lane = jnp.tile(d_skip.reshape(GB, 1, LANES), (1, 1, T))

    KU, KS = T * LANES, 4 * gl * P
    dims = dict(T=T, gl=gl, Cg=Cg, P=P)
    ds = pl.pallas_call(
        functools.partial(_s5_state_in_body, **dims),
        out_shape=jax.ShapeDtypeStruct((2, 2, rows, G * P), F32),
        grid=(GB, rows // rt),
        in_specs=u_specs + [pl.BlockSpec((None, KU, 4 * P), lambda gb, r: (gb, 0, 0))],
        out_specs=pl.BlockSpec((2, 2, rt, gl * P), lambda gb, r: (0, 0, r, gb)),
        scratch_shapes=[pltpu.VMEM((KU, KS), BF16)],
        compiler_params=_cparams("parallel", "arbitrary"),
        name="s5_state_in",
    )(u, a_in)

    lb = _pick(G * P, (1024, 512, 256, 128))
    s_start = pl.pallas_call(
        functools.partial(_s5_scan_body, B=B, nc_lat=nc_lat, nc_ctx=nc_ctx),
        out_shape=jax.ShapeDtypeStruct((2, 2, rows, G * P), F32),
        grid=(2, G * P // lb),
        in_specs=[pl.BlockSpec((None, 2, rows, lb), lambda d, j: (d, 0, 0, j)),
                  pl.BlockSpec((None, 2, 1, lb), lambda d, j: (d, 0, 0, j))],
        out_specs=pl.BlockSpec((None, 2, rows, lb), lambda d, j: (d, 0, 0, j)),
        compiler_params=_cparams("parallel", "parallel"),
        name="s5_scan",
    )(ds, a_tab)

    lag_spec = pl.BlockSpec((None, gl * Cg, T * Cg), lambda gb, r: (gb, 0, 0))
    y = pl.pallas_call(
        functools.partial(_s5_out_body, **dims),
        out_shape=jax.ShapeDtypeStruct((geo.R, D), F32),
        grid=(GB, rows // rt),
        in_specs=u_specs + [pl.BlockSpec((2, 2, rt, gl * P), lambda gb, r: (0, 0, r, gb)),
                            lag_spec, lag_spec,
                            pl.BlockSpec((None, KS, T * Cg), lambda gb, r: (gb, 0, 0)),
                            pl.BlockSpec((None, 1, KU), lambda gb, r: (gb, 0, 0))],
        out_specs=pl.BlockSpec((rt * T, LANES), lambda gb, r: (r, gb)),
        scratch_shapes=[pltpu.VMEM((KU + KS, KU), BF16), pltpu.VMEM((KU, T * Cg), BF16)],
        compiler_params=_cparams("parallel", "arbitrary"),
        name="s5_out",
    )(u, s_start, kf, kr, a_read, d_lane)

    lhs = [(y, _row_full(geo, D))]
    epi = [(xc, _row_tile(geo)), (modp, _mod_tile(geo, layer, 2))]
    (out,) = fused_matmul("s5_glu", geo, geo.tiles(True), lhs, w_glu.astype(BF16), 2, epi,
                          _identity_prologue, _glu_epilogue, [F32], alias={len(lhs) + 2: 0})
    return out


def _rope_tables(geo, head_dim):
    n_freq = head_dim // 4
    pos = jnp.arange(geo.L)
    inv_freq = ROPE_THETA ** (-jnp.arange(n_freq, dtype=F32) / n_freq)
    d = jnp.arange(LANES) % head_dim
    axis, half, f = d // (2 * n_freq), (d % (2 * n_freq)) // n_freq, d % n_freq
    p = jnp.where(axis[None, :] == 0, (pos // GRID_W)[:, None], (pos % GRID_W)[:, None]).astype(F32)
    ang = p * inv_freq[f][None, :]
    cos = jnp.cos(ang)
    sin = jnp.sin(ang) * jnp.where(half == 0, -1.0, 1.0)[None, :]
    rest = geo.R - geo.RL
    return (jnp.concatenate([jnp.tile(cos, (geo.B, 1)), jnp.ones((rest, LANES), F32)], axis=0),
            jnp.concatenate([jnp.tile(sin, (geo.B, 1)), jnp.zeros((rest, LANES), F32)], axis=0))


def _qkv_epilogue(accs, cos_ref, sin_ref, *, scale, n_freq):
    q, k, v = accs
    reps = q.shape[1] // LANES
    cos = jnp.concatenate([cos_ref[...]] * reps, axis=1)
    sin = jnp.concatenate([sin_ref[...]] * reps, axis=1)
    tn = q.shape[1]
    lane = lax.broadcasted_iota(jnp.int32, (1, tn), 1)
    first_half = (lane % (2 * n_freq)) < n_freq

    def rope(x):
        partner = jnp.where(first_half, pltpu.roll(x, tn - n_freq, axis=1), pltpu.roll(x, n_freq, axis=1))
        return x * cos + partner * sin

    return [rope(q) * scale, rope(k), v]


def _attn_body(q_ref, kl_ref, kc_ref, vl_ref, vc_ref, lam_ref, g_ref, o_ref, k_all, v_aug, *, lambda_init):
    q = q_ref[...]
    hw = q.shape[1]
    head_dim = hw // 2
    L = kl_ref.shape[0]
    lane = lax.broadcasted_iota(jnp.int32, (1, hw), 1)
    nt = (((1,), (1,)), ((), ()))

    @pl.when(pl.program_id(2) == 0)
    def _():
        k_all[:L, :] = kl_ref[...]
        k_all[L:, :] = kc_ref[...]
        v_aug[:L, :hw] = vl_ref[...]
        v_aug[L:, :hw] = vc_ref[...]
        v_aug[:, hw:] = jnp.ones((v_aug.shape[0], hw), v_aug.dtype)

    def attend(qm):
        s = lax.dot_general(qm, k_all[...], nt, preferred_element_type=F32)
        p = jnp.exp2(s - jnp.max(s, axis=1, keepdims=True)).astype(BF16)
        o = jnp.dot(p, v_aug[...], preferred_element_type=F32)
        return o[:, :hw] / o[:, hw:hw + 1]

    lp = lam_ref[...]
    lam = (jnp.exp(jnp.sum(lp[0:1] * lp[1:2], axis=1, keepdims=True))
           - jnp.exp(jnp.sum(lp[2:3] * lp[3:4], axis=1, keepdims=True)) + lambda_init)
    rows = q.shape[0] // ATTN_SUBTILES
    for r in range(ATTN_SUBTILES):
        qr = q[r * rows:(r + 1) * rows]
        o = (attend(jnp.where(lane < head_dim, qr, jnp.zeros_like(qr)))
             - lam * attend(jnp.where(lane >= head_dim, qr, jnp.zeros_like(qr))))
        o = o * lax.rsqrt(jnp.mean(o * o, axis=-1, keepdims=True) + EPS) * g_ref[...]
        o_ref[r * rows:(r + 1) * rows, :] = (o * (1.0 - lambda_init)).astype(o_ref.dtype)


def attn_mixer(geo, xc, modp, layer, gain, w_qkv, lam_params, subln_g, w_o):
    B, L, Lc, D = geo.B, geo.L, geo.Lc, geo.D
    head_dim = lam_params.shape[1]
    hw = 2 * head_dim
    assert hw == LANES and geo.R % Lc == 0
    H = D // hw
    lambda_init = 0.8 - 0.6 * math.exp(-0.3 * layer)
    cos, sin = _rope_tables(geo, head_dim)
    tab = pl.BlockSpec((geo.tm, LANES), lambda i, j: (i, 0))
    q, k, v = fused_matmul(
        "attn_qkv", geo, geo.tiles(True), _norm_mod_inputs(geo, xc, gain, modp, layer, 0),
        w_qkv.astype(BF16), 3, [(cos, tab), (sin, tab)], _norm_mod,
        functools.partial(_qkv_epilogue, scale=head_dim ** -0.5 * math.log2(math.e),
                          n_freq=head_dim // 4), [BF16] * 3)
    tq = _pick(L, (2048, 1024, 512, 256, 128))
    nq = L // tq
    ctx_blk = geo.RL // Lc
    lat_kv = pl.BlockSpec((L, hw), lambda b, h, i: (b, h))
    ctx_kv = pl.BlockSpec((Lc, hw), lambda b, h, i: (ctx_blk + b, h))
    o = pl.pallas_call(
        functools.partial(_attn_body, lambda_init=lambda_init),
        out_shape=jax.ShapeDtypeStruct((geo.R, D), BF16),
        grid=(B, H, nq),
        in_specs=[pl.BlockSpec((tq, hw), lambda b, h, i: (b * nq + i, h)),
                  lat_kv, ctx_kv, lat_kv, ctx_kv,
                  pl.BlockSpec((4, head_dim), lambda b, h, i: (0, 0)),
                  pl.BlockSpec((1, hw), lambda b, h, i: (0, 0))],
        out_specs=pl.BlockSpec((tq, hw), lambda b, h, i: (b * nq + i, h)),
        scratch_shapes=[pltpu.VMEM((L + Lc, hw), BF16), pltpu.VMEM((L + Lc, 2 * hw), BF16)],
        compiler_params=_cparams("parallel", "parallel", "arbitrary"),
        name="diff_attention",
    )(q, k, k, v, v, lam_params, subln_g.reshape(1, hw))
    lhs = [(o, _row_full(geo, D))]
    gw = geo.whole_width()
    epi = [(xc, _row_tile(gw)), (modp, _mod_tile(gw, layer, 2))]
    (out,) = fused_matmul("attn_out", gw, geo.tiles(False), lhs, w_o.astype(BF16), 1, epi,
                          _identity_prologue, _residual_epilogue, [F32], alias={len(lhs) + 1: 0})
    return out


def _router_body(x_ref, g_ref, sh_ref, sc_ref, rt_ref, hf_ref, aff_ref):
    h = _norm_mod(x_ref, g_ref, sh_ref, sc_ref)
    half = h.shape[1] // 2
    hb = h.astype(BF16)
    hbf = hb.astype(F32)
    hi = pltpu.bitcast(hbf[:, :half], jnp.uint32)
    lo = lax.shift_right_logical(pltpu.bitcast(hbf[:, half:], jnp.uint32), jnp.uint32(16))
    hf_ref[...] = hi | lo
    h_lo = (h - hbf).astype(BF16)
    rt = rt_ref[...]
    r_hi = rt.astype(BF16)
    r_lo = (rt - r_hi.astype(F32)).astype(BF16)
    nt = (((1,), (1,)), ((), ()))
    logits = (lax.dot_general(r_hi, hb, nt, preferred_element_type=F32)
              + lax.dot_general(r_hi, h_lo, nt, preferred_element_type=F32)
              + lax.dot_general(r_lo, hb, nt, preferred_element_type=F32))
    m = jnp.max(logits, axis=0, keepdims=True)
    e = jnp.exp(logits - m)
    aff_ref[...] = e / jnp.sum(e, axis=0, keepdims=True)


def moe_router(geo, xc, modp, layer, gain, router, with_ctx):
    D, tm = geo.D, geo.tm
    E = router.shape[1]
    i_only = lambda f: (lambda i: f(i, 0))
    specs = [s for _, s in _norm_mod_inputs(geo, xc, gain, modp, layer, 3)]
    specs = [pl.BlockSpec(s.block_shape, i_only(s.index_map)) for s in specs]
    return pl.pallas_call(
        _router_body,
        out_shape=[jax.ShapeDtypeStruct((geo.R, D // 2), jnp.uint32),
                   jax.ShapeDtypeStruct((E, geo.R), F32)],
        grid=(geo.tiles(with_ctx),),
        in_specs=specs + [pl.BlockSpec((E, D), lambda i: (0, 0))],
        out_specs=[pl.BlockSpec((tm, D // 2), lambda i: (i, 0)),
                   pl.BlockSpec((E, tm), lambda i: (0, i))],
        compiler_params=_cparams("parallel"),
        name="moe_router",
    )(xc, gain.reshape(1, D), modp, modp, router.T)


def _prefix_count(x, tri):
    n = x.shape[1]
    off = jnp.zeros((x.shape[0], 1), F32)
    parts = []
    for j in range(n // LANES):
        blk = x[:, j * LANES:(j + 1) * LANES]
        inc = jnp.dot(blk.astype(BF16), tri, preferred_element_type=F32)
        parts.append(inc - blk + off)
        off = off + inc[:, LANES - 1:LANES]
    return jnp.concatenate(parts, axis=1)


def _select_body(aff_ref, idx_ref, g_ref, *, cap):
    a = aff_ref[...]
    E, n = a.shape
    bits = pltpu.bitcast(a, jnp.int32)
    thr = jnp.zeros((E, 1), jnp.int32)
    for k in range(30, -1, -1):
        cand = thr | jnp.int32(1 << k)
        cnt = jnp.sum((bits >= cand).astype(F32), axis=1, keepdims=True)
        thr = jnp.where(cnt >= cap, cand, thr)
    gt = bits > thr
    eq = bits == thr
    need = cap - jnp.sum(gt.astype(F32), axis=1, keepdims=True)
    r = lax.broadcasted_iota(jnp.int32, (LANES, LANES), 0)
    c = lax.broadcasted_iota(jnp.int32, (LANES, LANES), 1)
    tri = (r <= c).astype(BF16)
    sel = gt | (eq & (_prefix_count(eq.astype(F32), tri) < need))
    self32 = sel.astype(F32)
    slot = _prefix_count(self32, tri)
    slot = jnp.where(sel, slot, -1.0)
    t = lax.broadcasted_iota(jnp.int32, (1, n), 1)
    t_hi = (t // 64).astype(F32)
    t_lo = (t % 64).astype(F32)
    a1 = a.astype(BF16).astype(F32)
    a2 = (a - a1).astype(BF16).astype(F32)
    a3 = (a - a1 - a2).astype(BF16).astype(F32)
    s_iota = lax.broadcasted_iota(jnp.int32, (cap, 1), 0).astype(F32)
    row = lax.broadcasted_iota(jnp.int32, (SUBLANES, 1), 0)
    nt = (((1,), (1,)), ((), ()))
    for e in range(E):
        onehot = jnp.where(slot[e:e + 1, :] == s_iota, 1.0, 0.0).astype(BF16)
        lhs = jnp.where(row == 0, t_hi, jnp.where(row == 1, t_lo, jnp.where(
            row == 2, a1[e:e + 1, :], jnp.where(row == 3, a2[e:e + 1, :], jnp.where(
                row == 4, a3[e:e + 1, :], 0.0))))).astype(BF16)
        res = lax.dot_general(lhs, onehot, nt, preferred_element_type=F32)
        idx_ref[e:e + 1, :] = (res[0:1, :] * 64.0 + res[1:2, :]).astype(jnp.int32)
        g_ref[e:e + 1, :] = res[2:3, :] + res[3:4, :] + res[4:5, :]


def moe_select(aff, n_samples, n, col_off):
    E = aff.shape[0]
    cap = max(1, CAPACITY_FACTOR * n // E)
    blk0 = col_off // n
    return pl.pallas_call(
        functools.partial(_select_body, cap=cap),
        out_shape=[jax.ShapeDtypeStruct((n_samples, E, cap), jnp.int32),
                   jax.ShapeDtypeStruct((n_samples, E, cap), F32)],
        grid=(n_samples,),
        in_specs=[pl.BlockSpec((E, n), lambda s: (0, blk0 + s))],
        out_specs=[pl.BlockSpec((None, E, cap), lambda s: (s, 0, 0)),
                   pl.BlockSpec((None, E, cap), lambda s: (s, 0, 0))],
        compiler_params=_cparams("parallel"),
        name="moe_select",
    )(aff)


def _gather_body(idx_ref, hf_ref, *rest, cap, n_exp):
    o_ref = rest[-1]
    base = (pl.program_id(0) * n_exp + pl.program_id(1)) * cap

    def step(i, carry):
        i0 = pl.multiple_of(i * SUBLANES, SUBLANES)
        rows = [hf_ref[pl.ds(idx_ref[base + i0 + k], 1), :] for k in range(SUBLANES)]
        for k in range(SUBLANES):
            o_ref[pl.ds(i0 + k, 1), :] = rows[k]
        return carry

    lax.fori_loop(0, cap // SUBLANES, step, 0)


def moe_gather(idx, hf, xg, rows_alloc, n, row_off, slot_off):
    S, E, cap = idx.shape
    W = hf.shape[1]
    rb, sb = row_off // n, slot_off // cap
    prev = [] if xg is None else [xg]
    grid_spec = pltpu.PrefetchScalarGridSpec(
        num_scalar_prefetch=1,
        grid=(S, E),
        in_specs=[pl.BlockSpec((n, W), lambda s, e, idx: (rb + s, 0))]
        + [pl.BlockSpec(memory_space=pl.ANY) for _ in prev],
        out_specs=pl.BlockSpec((None, cap, W), lambda s, e, idx: (e, sb + s, 0)),
    )
    return pl.pallas_call(
        functools.partial(_gather_body, cap=cap, n_exp=E),
        out_shape=jax.ShapeDtypeStruct((E, rows_alloc, W), hf.dtype),
        grid_spec=grid_spec,
        input_output_aliases={2: 0} if prev else {},
        compiler_params=_cparams("parallel", "arbitrary"),
        name="moe_gather",
    )(idx.reshape(-1), hf, *prev)


def _expert_body(xg_ref, g_ref, wg_ref, wu_ref, wd_ref, y_ref, wg_s, wu_s, wd_s, *, n_exp):
    j, k = pl.program_id(0), pl.program_id(1)

    @pl.when(j < n_exp)
    def _():
        dk, fk = wg_ref.shape[0], wd_ref.shape[0]
        rows_d = pl.ds(pl.multiple_of(k * dk, dk), dk)
        wg_s[j % 2, rows_d, :] = wg_ref[...].astype(BF16)
        wu_s[j % 2, rows_d, :] = wu_ref[...].astype(BF16)
        wd_s[j % 2, pl.ds(pl.multiple_of(k * fk, fk), fk), :] = wd_ref[...].astype(BF16)

    @pl.when(j == 0)
    def _():
        y_ref[...] = jnp.zeros_like(y_ref)

    @pl.when(j > 0)
    def _():
        cur = (j + 1) % 2
        u = xg_ref[...]
        half = u.shape[1]
        a = pltpu.bitcast(u & jnp.uint32(0xFFFF0000), F32).astype(BF16)
        b = pltpu.bitcast(lax.shift_left(u, jnp.uint32(16)), F32).astype(BF16)
        gate = (jnp.dot(a, wg_s[cur, :half, :], preferred_element_type=F32)
                + jnp.dot(b, wg_s[cur, half:, :], preferred_element_type=F32))
        up = (jnp.dot(a, wu_s[cur, :half, :], preferred_element_type=F32)
              + jnp.dot(b, wu_s[cur, half:, :], preferred_element_type=F32))
        act = (gate * jax.nn.sigmoid(gate) * up).astype(BF16)
        y_ref[...] = jnp.dot(act, wd_s[cur], preferred_element_type=F32) * g_ref[...]


def moe_experts(xg, g_col, rows, layer, w_gate, w_up, w_down):
    E, rows_alloc, W = xg.shape
    D, F = w_gate.shape[2], w_gate.shape[3]
    tr = max(t for t in range(16, 513, 16) if rows % t == 0)
    K = rows // tr
    assert D % (16 * K) == 0 and F % (16 * K) == 0
    w_exp = lambda j: jnp.minimum(j, E - 1)
    x_exp = lambda j: jnp.maximum(j - 1, 0)
    y_exp = lambda j: jnp.where(j == 0, E, j - 1)
    return pl.pallas_call(
        functools.partial(_expert_body, n_exp=E),
        out_shape=jax.ShapeDtypeStruct((E + 1, rows_alloc, D), F32),
        grid=(E + 1, K),
        in_specs=[pl.BlockSpec((None, tr, W), lambda j, k: (x_exp(j), k, 0)),
                  pl.BlockSpec((None, tr, 1), lambda j, k: (x_exp(j), k, 0)),
                  pl.BlockSpec((None, None, D // K, F), lambda j, k: (layer, w_exp(j), k, 0)),
                  pl.BlockSpec((None, None, D // K, F), lambda j, k: (layer, w_exp(j), k, 0)),
                  pl.BlockSpec((None, None, F // K, D), lambda j, k: (layer, w_exp(j), k, 0))],
        out_specs=pl.BlockSpec((None, tr, D), lambda j, k: (y_exp(j), k, 0)),
        scratch_shapes=[pltpu.VMEM((2, D, F), BF16), pltpu.VMEM((2, D, F), BF16),
                        pltpu.VMEM((2, F, D), BF16)],
        compiler_params=_cparams("arbitrary", "arbitrary"),
        name="moe_experts",
    )(xg, g_col, w_gate, w_up, w_down)


def _combine_body(idx_ref, y_ref, x_ref, gate_ref, o_ref, *, cap, n_exp):
    e = pl.program_id(2)
    base = (pl.program_id(0) * n_exp + e) * cap

    @pl.when(e == 0)
    def _():
        o_ref[...] = jnp.zeros_like(o_ref)

    def step(i, carry):
        i0 = pl.multiple_of(i * SUBLANES, SUBLANES)
        y8 = y_ref[pl.ds(i0, SUBLANES), :]
        tok = [idx_ref[base + i0 + k] for k in range(SUBLANES)]
        acc = [o_ref[pl.ds(tok[k], 1), :] for k in range(SUBLANES)]
        for k in range(SUBLANES):
            o_ref[pl.ds(tok[k], 1), :] = acc[k] + y8[k:k + 1, :]
        return carry

    lax.fori_loop(0, cap // SUBLANES, step, 0)

    @pl.when(e == n_exp - 1)
    def _():
        o_ref[...] = x_ref[...] + gate_ref[...] * o_ref[...]


def moe_combine(geo, idx, y, xc, modp, layer, n, row_off, slot_off, mod_ctx):
    S, E, cap = idx.shape
    D = geo.D
    dc = _pick(D, (1024, 512, 256, 128))
    rb, sb = row_off // n, slot_off // cap
    mod_row = (lambda s: geo.B) if mod_ctx else (lambda s: s)
    once = pl.Buffered(1)
    grid_spec = pltpu.PrefetchScalarGridSpec(
        num_scalar_prefetch=1,
        grid=(S, D // dc, E),
        in_specs=[pl.BlockSpec((None, cap, dc), lambda s, c, e, idx: (e, sb + s, c)),
                  pl.BlockSpec((n, dc), lambda s, c, e, idx: (rb + s, c), pipeline_mode=once),
                  pl.BlockSpec((None, None, None, 1, dc),
                               lambda s, c, e, idx: (layer, mod_row(s), 5, 0, c))],
        out_specs=pl.BlockSpec((n, dc), lambda s, c, e, idx: (rb + s, c), pipeline_mode=once),
    )
    return pl.pallas_call(
        functools.partial(_combine_body, cap=cap, n_exp=E),
        out_shape=jax.ShapeDtypeStruct(xc.shape, xc.dtype),
        grid_spec=grid_spec,
        input_output_aliases={2: 0},
        compiler_params=_cparams("parallel", "parallel", "arbitrary"),
        name="moe_combine",
    )(idx.reshape(-1), y, xc, modp)


def moe_layer(geo, xc, modp, layer, gain, router, w_gate, w_up, w_down, with_ctx):
    B, L, Lc = geo.B, geo.L, geo.Lc
    E = router.shape[1]
    hf, aff = moe_router(geo, xc, modp, layer, gain, router, with_ctx)
    groups = [(L, 0)] + ([(Lc, geo.RL)] if with_ctx else [])
    sel = [moe_select(aff, B, n, off) for n, off in groups]
    slot_offs = [0, B * sel[0][0].shape[2]]
    rows = sum(B * s[0].shape[2] for s in sel)
    cap_l = sel[0][0].shape[2]
    rows_alloc = -(-rows // cap_l) * cap_l
    xg = None
    for (n, off), (idx, _), so in zip(groups, sel, slot_offs):
        xg = moe_gather(idx, hf, xg, rows_alloc, n, off, so)
    g_col = jnp.concatenate([g.transpose(1, 0, 2).reshape(E, -1) for _, g in sel]
                            + [jnp.zeros((E, rows_alloc - rows), F32)], axis=1)[:, :, None]
    y = moe_experts(xg, g_col, rows, layer, w_gate, w_up, w_down)
    for k, ((n, off), (idx, _), so) in enumerate(zip(groups, sel, slot_offs)):
        xc = moe_combine(geo, idx, y, xc, modp, layer, n, off, so, mod_ctx=k == 1)
    return xc


def _final_norm_body(x_ref, g_ref, o_ref):
    x = x_ref[...]
    o_ref[...] = x * lax.rsqrt(jnp.mean(x * x, axis=-1, keepdims=True) + EPS) * g_ref[...]


def final_norm(geo, xc, gain):
    D, tm = geo.D, geo.tm
    return pl.pallas_call(
        _final_norm_body,
        out_shape=jax.ShapeDtypeStruct((geo.RL, D), F32),
        grid=(geo.RL // tm,),
        in_specs=[pl.BlockSpec((tm, D), lambda i: (i, 0)), pl.BlockSpec((1, D), lambda i: (0, 0))],
        out_specs=pl.BlockSpec((tm, D), lambda i: (i, 0)),
        compiler_params=_cparams("parallel"),
        name="final_norm",
    )(xc, gain.reshape(1, D))


def kernel(x, c, ctx, c_ctx, ada_w, ada_b, norm_mix_g, norm_ffn_g, final_norm_g, conv_w_in, conv_w, conv_w_out, ssm_w_in, ssm_a_re, ssm_a_im, ssm_log_dt, ssm_b_re, ssm_b_im, ssm_c_re, ssm_c_im, ssm_d, ssm_w_glu, attn_w_qkv, attn_lambda, attn_subln_g, attn_w_o, moe_router, moe_w_gate, moe_w_up, moe_w_down):
    B, L, D = x.shape
    Lc = ctx.shape[1]
    depth = ada_w.shape[0]
    geo = Geo(B, L, Lc, D)
    xc = jnp.concatenate([x.reshape(B * L, D), ctx.reshape(B * Lc, D),
                          jnp.zeros((geo.R - geo.Rv, D), F32)], axis=0)
    c_all = jnp.zeros((SUBLANES, D), F32).at[:B].set(c).at[B].set(c_ctx)
    modp = ada_modulation(c_all, ada_w, ada_b).reshape(depth, SUBLANES, 6, 1, D)
    ctx_live = [any(MIXER_READS_CTX[j % N_MIXERS] for j in range(i + 1, depth)) for i in range(depth)]
    for i in range(depth):
        kind, slot = i % N_MIXERS, i // N_MIXERS
        ctx_out = ctx_live[i]
        if kind == 0:
            xc = conv_mixer(geo, xc, modp, i, norm_mix_g[i], conv_w_in[slot], conv_w[slot],
                            conv_w_out[slot], ctx_out)
        elif kind == 1:
            xc = s5_mixer(geo, xc, modp, i, norm_mix_g[i], ssm_w_in[slot], ssm_a_re[slot], ssm_a_im[slot],
                          ssm_log_dt[slot], ssm_b_re[slot], ssm_b_im[slot], ssm_c_re[slot], ssm_c_im[slot],
                          ssm_d[slot], ssm_w_glu[slot])
        else:
            xc = attn_mixer(geo, xc, modp, i, norm_mix_g[i], attn_w_qkv[slot], attn_lambda[slot],
                            attn_subln_g[slot], attn_w_o[slot])
        xc = moe_layer(geo, xc, modp, i, norm_ffn_g[i], moe_router[i], moe_w_gate, moe_w_up, moe_w_down,
                       ctx_out)
    return final_norm(geo, xc, final_norm_g).reshape(B, L, D)
```

```python
import functools
import math

import jax
import jax.numpy as jnp
from jax import lax
from jax.experimental import pallas as pl
from jax.experimental.pallas import tpu as pltpu

F32 = jnp.float32
BF16 = jnp.bfloat16
EPS = 1e-6
N_MIXERS = 3
MIXER_READS_CTX = (False, True, True)
GRID_W = 64
ROPE_THETA = 10000.0
CAPACITY_FACTOR = 2
SSM_CHUNK = 16
SUBLANES = 8
CONV_HALO = 16
ATTN_SUBTILES = 8
LANES = 128
VMEM_LIMIT = 56 * 1024 * 1024


def _cparams(*sem):
    return pltpu.CompilerParams(dimension_semantics=sem, vmem_limit_bytes=VMEM_LIMIT)


def _pick(n, cands):
    for c in cands:
        if n % c == 0:
            return c
    raise ValueError(f"no tile in {cands} divides {n}")


class Geo:
    def __init__(self, B, L, Lc, D, tm=None, tn=None):
        self.B, self.L, self.Lc, self.D = B, L, Lc, D
        self.RL = B * L
        self.Rv = B * (L + Lc)
        self.R = -(-self.Rv // L) * L
        self.tm = tm or _pick(math.gcd(L, B * Lc), (512, 256, 128, 64, 32, 16, 8))
        assert L % self.tm == 0 and (B * Lc) % self.tm == 0
        self.tn = tn or _pick(D, (1024, 512, 256, 128))

    def whole_width(self):
        return Geo(self.B, self.L, self.Lc, self.D, tm=self.tm, tn=self.D)

    def mod_row(self, i):
        return jnp.minimum((i * self.tm) // self.L, self.B)

    def tiles(self, with_ctx):
        return (self.Rv if with_ctx else self.RL) // self.tm


def _ada_body(c_ref, w_ref, b_ref, o_ref):
    c = c_ref[...]
    s = (c * jax.nn.sigmoid(c)).astype(BF16)
    o_ref[...] = jnp.dot(s, w_ref[...].astype(BF16), preferred_element_type=F32) + b_ref[...]


def ada_modulation(c_all, ada_w, ada_b):
    depth, D, N = ada_w.shape
    tn = _pick(N, (1024, 512, 256, 128))
    return pl.pallas_call(
        _ada_body,
        out_shape=jax.ShapeDtypeStruct((depth, SUBLANES, N), F32),
        grid=(depth, N // tn),
        in_specs=[pl.BlockSpec((SUBLANES, D), lambda l, j: (0, 0)),
                  pl.BlockSpec((None, D, tn), lambda l, j: (l, 0, j)),
                  pl.BlockSpec((None, 1, tn), lambda l, j: (l, 0, j))],
        out_specs=pl.BlockSpec((None, SUBLANES, tn), lambda l, j: (l, 0, j)),
        compiler_params=_cparams("parallel", "parallel"),
        name="ada_modulation",
    )(c_all, ada_w, ada_b.reshape(depth, 1, N))


def _mm_body(*refs, n_lhs, n_w, n_epi, prologue, epilogue):
    lhs = refs[:n_lhs]
    ws = refs[n_lhs:n_lhs + n_w]
    epi = refs[n_lhs + n_w:n_lhs + n_w + n_epi]
    outs = refs[n_lhs + n_w + n_epi:-1]
    hb = refs[-1]

    @pl.when(pl.program_id(1) == 0)
    def _():
        hb[...] = prologue(*lhs).astype(hb.dtype)

    h = hb[...]
    accs = [jnp.dot(h, w[...], preferred_element_type=F32) for w in ws]
    for o, v in zip(outs, epilogue(accs, *epi)):
        o[...] = v.astype(o.dtype).reshape(o.shape)


def fused_matmul(name, geo, n_tiles, lhs, w, n_split, epi, prologue, epilogue, outs, alias=None,
                 out_layout=None):
    K = w.shape[0]
    N = w.shape[1] // n_split
    tm, tn = geo.tm, geo.tn
    nj = N // tn
    w_specs = [pl.BlockSpec((K, tn), functools.partial(lambda i, j, s: (0, s * nj + j), s=s))
               for s in range(n_split)]
    arrays = [a for a, _ in lhs] + [w] * n_split + [a for a, _ in epi]
    specs = [s for _, s in lhs] + w_specs + [s for _, s in epi]
    body = functools.partial(_mm_body, n_lhs=len(lhs), n_w=n_split, n_epi=len(epi),
                             prologue=prologue, epilogue=epilogue)
    out_shape, out_spec = out_layout or ((geo.R, N), pl.BlockSpec((tm, tn), lambda i, j: (i, j)))
    res = pl.pallas_call(
        body,
        out_shape=[jax.ShapeDtypeStruct(out_shape, dt) for dt in outs],
        grid=(n_tiles, nj),
        in_specs=specs,
        out_specs=[out_spec for _ in outs],
        scratch_shapes=[pltpu.VMEM((tm, K), BF16)],
        input_output_aliases=alias or {},
        compiler_params=_cparams("parallel", "arbitrary"),
        name=name,
    )(*arrays)
    return res


def _row_full(geo, K):
    return pl.BlockSpec((geo.tm, K), lambda i, j: (i, 0))


def _row_tile(geo):
    return pl.BlockSpec((geo.tm, geo.tn), lambda i, j: (i, j))


def _vec_full(K):
    return pl.BlockSpec((1, K), lambda i, j: (0, 0))


def _mod_full(geo, layer, k):
    return pl.BlockSpec((None, None, None, 1, geo.D),
                        lambda i, j: (layer, geo.mod_row(i), k, 0, 0))


def _mod_tile(geo, layer, k):
    return pl.BlockSpec((None, None, None, 1, geo.tn),
                        lambda i, j: (layer, geo.mod_row(i), k, 0, j))


def _norm_mod(x_ref, g_ref, sh_ref, sc_ref):
    x = x_ref[...]
    y = x * lax.rsqrt(jnp.mean(x * x, axis=-1, keepdims=True) + EPS)
    return (y * g_ref[...]) * (1.0 + sc_ref[...]) + sh_ref[...]


def _norm_mod_inputs(geo, xc, gain, modp, layer, k_shift):
    return [(xc, _row_full(geo, geo.D)), (gain.reshape(1, geo.D), _vec_full(geo.D)),
            (modp, _mod_full(geo, layer, k_shift)), (modp, _mod_full(geo, layer, k_shift + 1))]


def _residual_epilogue(accs, x_ref, gate_ref):
    return [x_ref[...] + gate_ref[...] * accs[0]]


def _conv_in_epilogue(accs):
    gate_b, gate_c, v = accs
    return [gate_b, gate_c * v]


def _conv_prologue(gb_ref, z_ref, zp_ref, zn_ref, wc_ref, *, geo):
    tm = geo.tm
    z = z_ref[...].astype(F32)
    t = lax.broadcasted_iota(jnp.int32, (tm, 1), 0)
    row = pl.program_id(0) * tm + t
    is_lat = row < geo.RL
    pos = jnp.where(is_lat, row % geo.L, (row - geo.RL) % geo.Lc)
    first = pos == 0
    last = pos == jnp.where(is_lat, geo.L - 1, geo.Lc - 1)
    halo_last = zp_ref[CONV_HALO - 1:CONV_HALO, :].astype(F32)
    z_prev = jnp.where(t == 0, halo_last, pltpu.roll(z, 1, axis=0))
    z_prev = jnp.where(first, 0.0, z_prev)
    z_next = jnp.where(t == tm - 1, zn_ref[0:1, :].astype(F32), pltpu.roll(z, tm - 1, axis=0))
    z_next = jnp.where(last, 0.0, z_next)
    wc = wc_ref[...]
    conv = wc[0:1, :] * z_prev + wc[1:2, :] * z + wc[2:3, :] * z_next
    return gb_ref[...].astype(F32) * conv


def conv_mixer(geo, xc, modp, layer, gain, w_in, w_conv, w_out, with_ctx):
    D, tm = geo.D, geo.tm
    n_tiles = geo.tiles(with_ctx)
    gb, z = fused_matmul("conv_in", geo, n_tiles,
                         _norm_mod_inputs(geo, xc, gain, modp, layer, 0),
                         w_in.astype(BF16), 3, [], _norm_mod, _conv_in_epilogue, [BF16, BF16])
    r8 = tm // CONV_HALO
    last_blk = geo.R // CONV_HALO - 1
    halo_prev = pl.BlockSpec((CONV_HALO, D), lambda i, j: (jnp.maximum(i * r8 - 1, 0), 0))
    halo_next = pl.BlockSpec((CONV_HALO, D), lambda i, j: (jnp.minimum((i + 1) * r8, last_blk), 0))
    lhs = [(gb, _row_full(geo, D)), (z, _row_full(geo, D)), (z, halo_prev), (z, halo_next),
           (w_conv, pl.BlockSpec((3, D), lambda i, j: (0, 0)))]
    gw = geo.whole_width()
    epi = [(xc, _row_tile(gw)), (modp, _mod_tile(gw, layer, 2))]
    (out,) = fused_matmul("conv_out", gw, n_tiles, lhs, w_out.astype(BF16), 1, epi,
                          functools.partial(_conv_prologue, geo=geo), _residual_epilogue, [F32],
                          alias={len(lhs) + 1: 0})
    return out


def _s5_operators(a_re, a_im, log_dt, b_re, b_im, c_re, c_im, T):
    G, P = a_re.shape[1:]
    Cg = b_re.shape[-1]
    k = jnp.arange(T + 1, dtype=F32)[:, None, None]
    mt, bst, cst, aT = [], [], [], []
    for d in range(2):
        dt = jnp.exp(log_dt[d])[:, None]
        mag = jnp.exp(k * dt * a_re[d])
        pw_r, pw_i = mag * jnp.cos(k * dt * a_im[d]), mag * jnp.sin(k * dt * a_im[d])
        den = a_re[d] * a_re[d] + a_im[d] * a_im[d]
        nr = (pw_r[1] - 1.0) * a_re[d] + pw_i[1] * a_im[d]
        ni = pw_i[1] * a_re[d] - (pw_r[1] - 1.0) * a_im[d]
        fr, fi = (nr / den)[..., None], (ni / den)[..., None]
        bb_r = fr * b_re[d] - fi * b_im[d]
        bb_i = fr * b_im[d] + fi * b_re[d]
        lags = slice(0, T) if d == 0 else slice(T - 1, None, -1)
        pg_r, pg_i = (v[lags].transpose(1, 0, 2)[:, :, None, :] for v in (pw_r, pw_i))
        w_r = (c_re[d][:, None] * pg_r - c_im[d][:, None] * pg_i).reshape(G, T * Cg, P)
        w_i = (c_re[d][:, None] * pg_i + c_im[d][:, None] * pg_r).reshape(G, T * Cg, P)
        kk = (jnp.einsum('gmp,gpc->gcm', w_r, bb_r, precision='high')
              - jnp.einsum('gmp,gpc->gcm', w_i, bb_i, precision='high'))
        mt.append(kk.reshape(G * Cg, T * Cg))
        e = jnp.arange(T - 1, -1, -1) if d == 0 else jnp.arange(T)
        er, ei = pw_r[e], pw_i[e]
        s_r = er[:, :, :, None] * bb_r[None] - ei[:, :, :, None] * bb_i[None]
        s_i = er[:, :, :, None] * bb_i[None] + ei[:, :, :, None] * bb_r[None]
        bst.append(tuple(v.transpose(1, 0, 3, 2).reshape(G, T * Cg, P) for v in (s_r, s_i)))
        e = jnp.arange(1, T + 1) if d == 0 else jnp.arange(T, 0, -1)
        er, ei = pw_r[e], pw_i[e]
        o_r = c_re[d][None] * er[:, :, None, :] - c_im[d][None] * ei[:, :, None, :]
        o_i = -(c_re[d][None] * ei[:, :, None, :] + c_im[d][None] * er[:, :, None, :])
        cst.append(tuple(v.transpose(1, 3, 0, 2).reshape(G, P, T * Cg) for v in (o_r, o_i)))
        aT.append((pw_r[T], pw_i[T]))
    return mt, bst, cst, aT


def _lane_tile_operators(bst, cst, T, gl):
    G, TC, P = bst[0][0].shape
    Cg = TC // T
    GB = G // gl
    a_in = jnp.concatenate([m for d in range(2) for m in bst[d]], axis=2)
    a_in = a_in.reshape(GB, gl, T, Cg, 4 * P).transpose(0, 2, 1, 3, 4).reshape(GB, T * gl * Cg, 4 * P)
    a_read = jnp.stack([m.reshape(GB, gl * P, TC) for d in range(2) for m in cst[d]], axis=1)
    return a_in.astype(BF16), a_read.reshape(GB, 4 * gl * P, TC).astype(BF16)


def _toeplitz_rows(kf_ref, kr_ref, a_ref, *, T, Cg):
    n, TC = kf_ref.shape
    kf, kr = kf_ref[...], kr_ref[...]
    lane = lax.broadcasted_iota(jnp.int32, (1, TC), 1)
    for x in range(T):
        right, left = x * Cg, (T - 1 - x) * Cg
        fwd = jnp.where(lane >= right, kf if right == 0 else pltpu.roll(kf, right, axis=1), 0.0)
        bwd = jnp.where(lane < TC - left, kr if left == 0 else pltpu.roll(kr, TC - left, axis=1), 0.0)
        a_ref[x * n:(x + 1) * n, :] = (fwd + bwd).astype(a_ref.dtype)


def _spread_block_diag(a_ref, w_ref, row0, *, gl, row_blk, col_blk):
    n_rows, k = a_ref.shape
    n_cols = k * gl
    kr = lax.broadcasted_iota(jnp.int32, (k, n_cols), 0)
    kc = lax.broadcasted_iota(jnp.int32, (k, n_cols), 1)
    rep = jnp.where((kr // col_blk == kc // (gl * col_blk)) & (kr % col_blk == kc % col_blk),
                    1.0, 0.0).astype(a_ref.dtype)
    col_group = (lax.broadcasted_iota(jnp.int32, (1, n_cols), 1) // col_blk) % gl
    step = 512 if n_rows % 512 == 0 else n_rows
    for r0 in range(0, n_rows, step):
        row_group = ((lax.broadcasted_iota(jnp.int32, (step, 1), 0) + r0) // row_blk) % gl
        full = jnp.dot(a_ref[r0:r0 + step, :], rep, preferred_element_type=F32)
        w_ref[row0 + r0:row0 + r0 + step, :] = jnp.where(row_group == col_group, full, 0.0).astype(w_ref.dtype)


def _chunk_rows(u_ref, T):
    n = u_ref.shape[0] // T
    return jnp.concatenate([u_ref[pl.ds(t, n, stride=T), :] for t in range(T)], axis=1)


def _s5_state_in_body(u_ref, a_ref, ds_ref, w_ref, *, T, gl, Cg, P):
    @pl.when(pl.program_id(1) == 0)
    def _():
        _spread_block_diag(a_ref, w_ref, 0, gl=gl, row_blk=Cg, col_blk=P)

    u = _chunk_rows(u_ref, T).astype(BF16)
    res = jnp.dot(u, w_ref[...], preferred_element_type=F32)
    w = res.shape[1] // 4
    for k in range(4):
        ds_ref[k // 2, k % 2] = res[:, k * w:(k + 1) * w]


def _s5_scan_body(ds_ref, a_ref, s_ref, *, B, nc_lat, nc_ctx):
    a_re, a_im = a_ref[0], a_ref[1]
    zero = jnp.zeros((B, ds_ref.shape[-1]), F32)

    def segment(first, n, reverse, state):
        def step(i, st):
            s_re, s_im = st
            k = first + ((n - 1 - i) if reverse else i)
            for b in range(B):
                s_ref[0, pl.ds(k + b * n, 1), :] = s_re[b:b + 1].astype(s_ref.dtype)
                s_ref[1, pl.ds(k + b * n, 1), :] = s_im[b:b + 1].astype(s_ref.dtype)
            d_re = jnp.concatenate([ds_ref[0, pl.ds(k + b * n, 1), :] for b in range(B)], axis=0)
            d_im = jnp.concatenate([ds_ref[1, pl.ds(k + b * n, 1), :] for b in range(B)], axis=0)
            return a_re * s_re - a_im * s_im + d_re, a_re * s_im + a_im * s_re + d_im
        return lax.fori_loop(0, n, step, state)

    for d in range(2):
        @pl.when(pl.program_id(0) == d)
        def _():
            st = segment(B * nc_lat, nc_ctx, d == 1, (zero, zero))
            segment(0, nc_lat, d == 1, st)


def _s5_out_body(u_ref, s_ref, kf_ref, kr_ref, ar_ref, d_ref, y_ref, w_ref, ai_s, *, T, gl, Cg, P):
    @pl.when(pl.program_id(1) == 0)
    def _():
        _toeplitz_rows(kf_ref, kr_ref, ai_s, T=T, Cg=Cg)
        _spread_block_diag(ai_s, w_ref, 0, gl=gl, row_blk=Cg, col_blk=Cg)
        _spread_block_diag(ar_ref, w_ref, ai_s.shape[0], gl=gl, row_blk=P, col_blk=Cg)

    u = _chunk_rows(u_ref, T)
    s = jnp.concatenate([s_ref[d, part] for d in range(2) for part in range(2)], axis=1)
    y = (jnp.dot(jnp.concatenate([u, s], axis=1).astype(BF16), w_ref[...], preferred_element_type=F32)
         + u * d_ref[...])
    z = jax.nn.gelu(y)
    n = z.shape[0]
    for t in range(T):
        y_ref[pl.ds(t, n, stride=T), :] = z[:, t * LANES:(t + 1) * LANES]


def _glu_epilogue(accs, x_ref, gate_ref):
    val, gate = accs
    return [x_ref[...] + gate_ref[...] * (val * jax.nn.sigmoid(gate))]


def _identity_prologue(z_ref):
    return z_ref[...]


def s5_mixer(geo, xc, modp, layer, gain, w_in, a_re, a_im, log_dt, b_re, b_im, c_re, c_im, d_skip, w_glu):
    B, L, Lc, D = geo.B, geo.L, geo.Lc, geo.D
    G, P = a_re.shape[1:]
    Cg, T = D // G, SSM_CHUNK
    gl = LANES // Cg
    GB = G // gl
    nc_ctx, nc_lat = Lc // T, L // T
    rows = geo.Rv // T
    rt = _pick(rows, (272, 256, 128, 64, 32, 16))
    (u,) = fused_matmul("s5_in", geo.whole_width(), geo.tiles(True),
                        _norm_mod_inputs(geo, xc, gain, modp, layer, 0),
                        w_in.astype(BF16), 1, [], _norm_mod, lambda accs: accs, [F32])
    u_specs = [pl.BlockSpec((rt * T, LANES), lambda gb, r: (r, gb))]

    kk, bst, cst, aT = _s5_operators(a_re, a_im, log_dt, b_re, b_im, c_re, c_im, T)
    a_in, a_read = _lane_tile_operators(bst, cst, T, gl)
    kf, kr = (m.reshape(GB, gl * Cg, T * Cg) for m in kk)
    a_tab = jnp.stack([jnp.stack([v.reshape(1, G * P) for v in aT[d]]) for d in range(2)])
    d_lane = jnp.tile(d_skip.reshape(GB, 1, LANES), (1, 1, T))

    KU, KS = T * LANES, 4 * gl * P
    dims = dict(T=T, gl=gl, Cg=Cg, P=P)
    ds = pl.pallas_call(
        functools.partial(_s5_state_in_body, **dims),
        out_shape=jax.ShapeDtypeStruct((2, 2, rows, G * P), F32),
        grid=(GB, rows // rt),
        in_specs=u_specs + [pl.BlockSpec((None, KU, 4 * P), lambda gb, r: (gb, 0, 0))],
        out_specs=pl.BlockSpec((2, 2, rt, gl * P), lambda gb, r: (0, 0, r, gb)),
        scratch_shapes=[pltpu.VMEM((KU, KS), BF16)],
        compiler_params=_cparams("parallel", "arbitrary"),
        name="s5_state_in",
    )(u, a_in)

    lb = _pick(G * P, (1024, 512, 256, 128))
    s_start = pl.pallas_call(
        functools.partial(_s5_scan_body, B=B, nc_lat=nc_lat, nc_ctx=nc_ctx),
        out_shape=jax.ShapeDtypeStruct((2, 2, rows, G * P), F32),
        grid=(2, G * P // lb),
        in_specs=[pl.BlockSpec((None, 2, rows, lb), lambda d, j: (d, 0, 0, j)),
                  pl.BlockSpec((None, 2, 1, lb), lambda d, j: (d, 0, 0, j))],
        out_specs=pl.BlockSpec((None, 2, rows, lb), lambda d, j: (d, 0, 0, j)),
        compiler_params=_cparams("parallel", "parallel"),
        name="s5_scan",
    )(ds, a_tab)

    lag_spec = pl.BlockSpec((None, gl * Cg, T * Cg), lambda gb, r: (gb, 0, 0))
    y = pl.pallas_call(
        functools.partial(_s5_out_body, **dims),
        out_shape=jax.ShapeDtypeStruct((geo.R, D), F32),
        grid=(GB, rows // rt),
        in_specs=u_specs + [pl.BlockSpec((2, 2, rt, gl * P), lambda gb, r: (0, 0, r, gb)),
                            lag_spec, lag_spec,
                            pl.BlockSpec((None, KS, T * Cg), lambda gb, r: (gb, 0, 0)),
                            pl.BlockSpec((None, 1, KU), lambda gb, r: (gb, 0, 0))],
        out_specs=pl.BlockSpec((rt * T, LANES), lambda gb, r: (r, gb)),
        scratch_shapes=[pltpu.VMEM((KU + KS, KU), BF16), pltpu.VMEM((KU, T * Cg), BF16)],
        compiler_params=_cparams("parallel", "arbitrary"),
        name="s5_out",
    )(u, s_start, kf, kr, a_read, d_lane)

    lhs = [(y, _row_full(geo, D))]
    epi = [(xc, _row_tile(geo)), (modp, _mod_tile(geo, layer, 2))]
    (out,) = fused_matmul("s5_glu", geo, geo.tiles(True), lhs, w_glu.astype(BF16), 2, epi,
                          _identity_prologue, _glu_epilogue, [F32], alias={len(lhs) + 2: 0})
    return out


def _rope_tables(geo, head_dim):
    n_freq = head_dim // 4
    pos = jnp.arange(geo.L)
    inv_freq = ROPE_THETA ** (-jnp.arange(n_freq, dtype=F32) / n_freq)
    d = jnp.arange(LANES) % head_dim
    axis, half, f = d // (2 * n_freq), (d % (2 * n_freq)) // n_freq, d % n_freq
    p = jnp.where(axis[None, :] == 0, (pos // GRID_W)[:, None], (pos % GRID_W)[:, None]).astype(F32)
    ang = p * inv_freq[f][None, :]
    cos = jnp.cos(ang)
    sin = jnp.sin(ang) * jnp.where(half == 0, -1.0, 1.0)[None, :]
    rest = geo.R - geo.RL
    return (jnp.concatenate([jnp.tile(cos, (geo.B, 1)), jnp.ones((rest, LANES), F32)], axis=0),
            jnp.concatenate([jnp.tile(sin, (geo.B, 1)), jnp.zeros((rest, LANES), F32)], axis=0))


def _qkv_epilogue(accs, cos_ref, sin_ref, *, scale, n_freq):
    q, k, v = accs
    reps = q.shape[1] // LANES
    cos = jnp.concatenate([cos_ref[...]] * reps, axis=1)
    sin = jnp.concatenate([sin_ref[...]] * reps, axis=1)
    tn = q.shape[1]
    lane = lax.broadcasted_iota(jnp.int32, (1, tn), 1)
    first_half = (lane % (2 * n_freq)) < n_freq

    def rope(x):
        partner = jnp.where(first_half, pltpu.roll(x, tn - n_freq, axis=1), pltpu.roll(x, n_freq, axis=1))
        return x * cos + partner * sin

    return [rope(q) * scale, rope(k), v]


def _attn_body(q_ref, kl_ref, kc_ref, vl_ref, vc_ref, lam_ref, g_ref, o_ref, k_all, v_aug, *, lambda_init):
    q = q_ref[...]
    hw = q.shape[1]
    head_dim = hw // 2
    L = kl_ref.shape[0]
    lane = lax.broadcasted_iota(jnp.int32, (1, hw), 1)
    nt = (((1,), (1,)), ((), ()))

    @pl.when(pl.program_id(2) == 0)
    def _():
        k_all[:L, :] = kl_ref[...]
        k_all[L:, :] = kc_ref[...]
        v_aug[:L, :hw] = vl_ref[...]
        v_aug[L:, :hw] = vc_ref[...]
        v_aug[:, hw:] = jnp.ones((v_aug.shape[0], hw), v_aug.dtype)

    def attend(qm):
        s = lax.dot_general(qm, k_all[...], nt, preferred_element_type=F32)
        p = jnp.exp2(s - jnp.max(s, axis=1, keepdims=True)).astype(BF16)
        o = jnp.dot(p, v_aug[...], preferred_element_type=F32)
        return o[:, :hw] / o[:, hw:hw + 1]

    lp = lam_ref[...]
    lam = (jnp.exp(jnp.sum(lp[0:1] * lp[1:2], axis=1, keepdims=True))
           - jnp.exp(jnp.sum(lp[2:3] * lp[3:4], axis=1, keepdims=True)) + lambda_init)
    rows = q.shape[0] // ATTN_SUBTILES
    for r in range(ATTN_SUBTILES):
        qr = q[r * rows:(r + 1) * rows]
        o = (attend(jnp.where(lane < head_dim, qr, jnp.zeros_like(qr)))
             - lam * attend(jnp.where(lane >= head_dim, qr, jnp.zeros_like(qr))))
        o = o * lax.rsqrt(jnp.mean(o * o, axis=-1, keepdims=True) + EPS) * g_ref[...]
        o_ref[r * rows:(r + 1) * rows, :] = (o * (1.0 - lambda_init)).astype(o_ref.dtype)


def attn_mixer(geo, xc, modp, layer, gain, w_qkv, lam_params, subln_g, w_o):
    B, L, Lc, D = geo.B, geo.L, geo.Lc, geo.D
    head_dim = lam_params.shape[1]
    hw = 2 * head_dim
    assert hw == LANES and geo.R % Lc == 0
    H = D // hw
    lambda_init = 0.8 - 0.6 * math.exp(-0.3 * layer)
    cos, sin = _rope_tables(geo, head_dim)
    tab = pl.BlockSpec((geo.tm, LANES), lambda i, j: (i, 0))
    q, k, v = fused_matmul(
        "attn_qkv", geo, geo.tiles(True), _norm_mod_inputs(geo, xc, gain, modp, layer, 0),
        w_qkv.astype(BF16), 3, [(cos, tab), (sin, tab)], _norm_mod,
        functools.partial(_qkv_epilogue, scale=head_dim ** -0.5 * math.log2(math.e),
                          n_freq=head_dim // 4), [BF16] * 3)
    tq = _pick(L, (2048, 1024, 512, 256, 128))
    nq = L // tq
    ctx_blk = geo.RL // Lc
    lat_kv = pl.BlockSpec((L, hw), lambda b, h, i: (b, h))
    ctx_kv = pl.BlockSpec((Lc, hw), lambda b, h, i: (ctx_blk + b, h))
    o = pl.pallas_call(
        functools.partial(_attn_body, lambda_init=lambda_init),
        out_shape=jax.ShapeDtypeStruct((geo.R, D), BF16),
        grid=(B, H, nq),
        in_specs=[pl.BlockSpec((tq, hw), lambda b, h, i: (b * nq + i, h)),
                  lat_kv, ctx_kv, lat_kv, ctx_kv,
                  pl.BlockSpec((4, head_dim), lambda b, h, i: (0, 0)),
                  pl.BlockSpec((1, hw), lambda b, h, i: (0, 0))],
        out_specs=pl.BlockSpec((tq, hw), lambda b, h, i: (b * nq + i, h)),
        scratch_shapes=[pltpu.VMEM((L + Lc, hw), BF16), pltpu.VMEM((L + Lc, 2 * hw), BF16)],
        compiler_params=_cparams("parallel", "parallel", "arbitrary"),
        name="diff_attention",
    )(q, k, k, v, v, lam_params, subln_g.reshape(1, hw))
    lhs = [(o, _row_full(geo, D))]
    gw = geo.whole_width()
    epi = [(xc, _row_tile(gw)), (modp, _mod_tile(gw, layer, 2))]
    (out,) = fused_matmul("attn_out", gw, geo.tiles(False), lhs, w_o.astype(BF16), 1, epi,
                          _identity_prologue, _residual_epilogue, [F32], alias={len(lhs) + 1: 0})
    return out


def _router_body(x_ref, g_ref, sh_ref, sc_ref, rt_ref, hf_ref, aff_ref):
    h = _norm_mod(x_ref, g_ref, sh_ref, sc_ref)
    half = h.shape[1] // 2
    hb = h.astype(BF16)
    hbf = hb.astype(F32)
    hi = pltpu.bitcast(hbf[:, :half], jnp.uint32)
    lo = lax.shift_right_logical(pltpu.bitcast(hbf[:, half:], jnp.uint32), jnp.uint32(16))
    hf_ref[...] = hi | lo
    h_lo = (h - hbf).astype(BF16)
    rt = rt_ref[...]
    r_hi = rt.astype(BF16)
    r_lo = (rt - r_hi.astype(F32)).astype(BF16)
    nt = (((1,), (1,)), ((), ()))
    logits = (lax.dot_general(r_hi, hb, nt, preferred_element_type=F32)
              + lax.dot_general(r_hi, h_lo, nt, preferred_element_type=F32)
              + lax.dot_general(r_lo, hb, nt, preferred_element_type=F32))
    m = jnp.max(logits, axis=0, keepdims=True)
    e = jnp.exp(logits - m)
    aff_ref[...] = e / jnp.sum(e, axis=0, keepdims=True)


def moe_router(geo, xc, modp, layer, gain, router, with_ctx):
    D, tm = geo.D, geo.tm
    E = router.shape[1]
    i_only = lambda f: (lambda i: f(i, 0))
    specs = [s for _, s in _norm_mod_inputs(geo, xc, gain, modp, layer, 3)]
    specs = [pl.BlockSpec(s.block_shape, i_only(s.index_map)) for s in specs]
    return pl.pallas_call(
        _router_body,
        out_shape=[jax.ShapeDtypeStruct((geo.R, D // 2), jnp.uint32),
                   jax.ShapeDtypeStruct((E, geo.R), F32)],
        grid=(geo.tiles(with_ctx),),
        in_specs=specs + [pl.BlockSpec((E, D), lambda i: (0, 0))],
        out_specs=[pl.BlockSpec((tm, D // 2), lambda i: (i, 0)),
                   pl.BlockSpec((E, tm), lambda i: (0, i))],
        compiler_params=_cparams("parallel"),
        name="moe_router",
    )(xc, gain.reshape(1, D), modp, modp, router.T)


def _prefix_count(x, tri):
    n = x.shape[1]
    off = jnp.zeros((x.shape[0], 1), F32)
    parts = []
    for j in range(n // LANES):
        blk = x[:, j * LANES:(j + 1) * LANES]
        inc = jnp.dot(blk.astype(BF16), tri, preferred_element_type=F32)
        parts.append(inc - blk + off)
        off = off + inc[:, LANES - 1:LANES]
    return jnp.concatenate(parts, axis=1)


def _select_body(aff_ref, idx_ref, g_ref, *, cap):
    a = aff_ref[...]
    E, n = a.shape
    bits = pltpu.bitcast(a, jnp.int32)
    thr = jnp.zeros((E, 1), jnp.int32)
    for k in range(30, -1, -1):
        cand = thr | jnp.int32(1 << k)
        cnt = jnp.sum((bits >= cand).astype(F32), axis=1, keepdims=True)
        thr = jnp.where(cnt >= cap, cand, thr)
    gt = bits > thr
    eq = bits == thr
    need = cap - jnp.sum(gt.astype(F32), axis=1, keepdims=True)
    r = lax.broadcasted_iota(jnp.int32, (LANES, LANES), 0)
    c = lax.broadcasted_iota(jnp.int32, (LANES, LANES), 1)
    tri = (r <= c).astype(BF16)
    sel = gt | (eq & (_prefix_count(eq.astype(F32), tri) < need))
    self32 = sel.astype(F32)
    slot = _prefix_count(self32, tri)
    slot = jnp.where(sel, slot, -1.0)
    t = lax.broadcasted_iota(jnp.int32, (1, n), 1)
    t_hi = (t // 64).astype(F32)
    t_lo = (t % 64).astype(F32)
    a1 = a.astype(BF16).astype(F32)
    a2 = (a - a1).astype(BF16).astype(F32)
    a3 = (a - a1 - a2).astype(BF16).astype(F32)
    s_iota = lax.broadcasted_iota(jnp.int32, (cap, 1), 0).astype(F32)
    row = lax.broadcasted_iota(jnp.int32, (SUBLANES, 1), 0)
    nt = (((1,), (1,)), ((), ()))
    for e in range(E):
        onehot = jnp.where(slot[e:e + 1, :] == s_iota, 1.0, 0.0).astype(BF16)
        lhs = jnp.where(row == 0, t_hi, jnp.where(row == 1, t_lo, jnp.where(
            row == 2, a1[e:e + 1, :], jnp.where(row == 3, a2[e:e + 1, :], jnp.where(
                row == 4, a3[e:e + 1, :], 0.0))))).astype(BF16)
        res = lax.dot_general(lhs, onehot, nt, preferred_element_type=F32)
        idx_ref[e:e + 1, :] = (res[0:1, :] * 64.0 + res[1:2, :]).astype(jnp.int32)
        g_ref[e:e + 1, :] = res[2:3, :] + res[3:4, :] + res[4:5, :]


def moe_select(aff, n_samples, n, col_off):
    E = aff.shape[0]
    cap = max(1, CAPACITY_FACTOR * n // E)
    blk0 = col_off // n
    return pl.pallas_call(
        functools.partial(_select_body, cap=cap),
        out_shape=[jax.ShapeDtypeStruct((n_samples, E, cap), jnp.int32),
                   jax.ShapeDtypeStruct((n_samples, E, cap), F32)],
        grid=(n_samples,),
        in_specs=[pl.BlockSpec((E, n), lambda s: (0, blk0 + s))],
        out_specs=[pl.BlockSpec((None, E, cap), lambda s: (s, 0, 0)),
                   pl.BlockSpec((None, E, cap), lambda s: (s, 0, 0))],
        compiler_params=_cparams("parallel"),
        name="moe_select",
    )(aff)


def _gather_body(idx_ref, hf_ref, *rest, cap, n_exp):
    o_ref = rest[-1]
    base = (pl.program_id(0) * n_exp + pl.program_id(1)) * cap

    def step(i, carry):
        i0 = pl.multiple_of(i * SUBLANES, SUBLANES)
        rows = [hf_ref[pl.ds(idx_ref[base + i0 + k], 1), :] for k in range(SUBLANES)]
        dst = o_ref.at[pl.ds(i0, SUBLANES), :]
        for k in range(SUBLANES):
            dst[k:k + 1, :] = rows[k]
        return carry

    lax.fori_loop(0, cap // SUBLANES, step, 0)


def moe_gather(idx, hf, xg, rows_alloc, n, row_off, slot_off):
    S, E, cap = idx.shape
    W = hf.shape[1]
    rb, sb = row_off // n, slot_off // cap
    prev = [] if xg is None else [xg]
    grid_spec = pltpu.PrefetchScalarGridSpec(
        num_scalar_prefetch=1,
        grid=(S, E),
        in_specs=[pl.BlockSpec((n, W), lambda s, e, idx: (rb + s, 0))]
        + [pl.BlockSpec(memory_space=pl.ANY) for _ in prev],
        out_specs=pl.BlockSpec((None, cap, W), lambda s, e, idx: (e, sb + s, 0)),
    )
    return pl.pallas_call(
        functools.partial(_gather_body, cap=cap, n_exp=E),
        out_shape=jax.ShapeDtypeStruct((E, rows_alloc, W), hf.dtype),
        grid_spec=grid_spec,
        input_output_aliases={2: 0} if prev else {},
        compiler_params=_cparams("parallel", "arbitrary"),
        name="moe_gather",
    )(idx.reshape(-1), hf, *prev)


def _expert_body(xg_ref, g_ref, wg_ref, wu_ref, wd_ref, y_ref, wg_s, wu_s, wd_s, *, n_exp):
    j, k = pl.program_id(0), pl.program_id(1)

    @pl.when(j < n_exp)
    def _():
        dk, fk = wg_ref.shape[0], wd_ref.shape[0]
        rows_d = pl.ds(pl.multiple_of(k * dk, dk), dk)
        wg_s[j % 2, rows_d, :] = wg_ref[...].astype(BF16)
        wu_s[j % 2, rows_d, :] = wu_ref[...].astype(BF16)
        wd_s[j % 2, pl.ds(pl.multiple_of(k * fk, fk), fk), :] = wd_ref[...].astype(BF16)

    @pl.when(j == 0)
    def _():
        y_ref[...] = jnp.zeros_like(y_ref)

    @pl.when(j > 0)
    def _():
        cur = (j + 1) % 2
        u = xg_ref[...]
        half = u.shape[1]
        a = pltpu.bitcast(u & jnp.uint32(0xFFFF0000), F32).astype(BF16)
        b = pltpu.bitcast(lax.shift_left(u, jnp.uint32(16)), F32).astype(BF16)
        gate = (jnp.dot(a, wg_s[cur, :half, :], preferred_element_type=F32)
                + jnp.dot(b, wg_s[cur, half:, :], preferred_element_type=F32))
        up = (jnp.dot(a, wu_s[cur, :half, :], preferred_element_type=F32)
              + jnp.dot(b, wu_s[cur, half:, :], preferred_element_type=F32))
        act = (gate * jax.nn.sigmoid(gate) * up).astype(BF16)
        y_ref[...] = jnp.dot(act, wd_s[cur], preferred_element_type=F32) * g_ref[...]


def moe_experts(xg, g_col, rows, layer, w_gate, w_up, w_down):
    E, rows_alloc, W = xg.shape
    D, F = w_gate.shape[2], w_gate.shape[3]
    tr = max(t for t in range(16, 513, 16) if rows % t == 0)
    K = rows // tr
    assert D % (16 * K) == 0 and F % (16 * K) == 0
    w_exp = lambda j: jnp.minimum(j, E - 1)
    x_exp = lambda j: jnp.maximum(j - 1, 0)
    y_exp = lambda j: jnp.where(j == 0, E, j - 1)
    return pl.pallas_call(
        functools.partial(_expert_body, n_exp=E),
        out_shape=jax.ShapeDtypeStruct((E + 1, rows_alloc, D), F32),
        grid=(E + 1, K),
        in_specs=[pl.BlockSpec((None, tr, W), lambda j, k: (x_exp(j), k, 0)),
                  pl.BlockSpec((None, tr, 1), lambda j, k: (x_exp(j), k, 0)),
                  pl.BlockSpec((None, None, D // K, F), lambda j, k: (layer, w_exp(j), k, 0)),
                  pl.BlockSpec((None, None, D // K, F), lambda j, k: (layer, w_exp(j), k, 0)),
                  pl.BlockSpec((None, None, F // K, D), lambda j, k: (layer, w_exp(j), k, 0))],
        out_specs=pl.BlockSpec((None, tr, D), lambda j, k: (y_exp(j), k, 0)),
        scratch_shapes=[pltpu.VMEM((2, D, F), BF16), pltpu.VMEM((2, D, F), BF16),
                        pltpu.VMEM((2, F, D), BF16)],
        compiler_params=_cparams("arbitrary", "arbitrary"),
        name="moe_experts",
    )(xg, g_col, w_gate, w_up, w_down)


def _combine_body(idx_ref, y_ref, x_ref, gate_ref, o_ref, *, cap, n_exp):
    e = pl.program_id(2)
    base = (pl.program_id(0) * n_exp + e) * cap

    @pl.when(e == 0)
    def _():
        o_ref[...] = jnp.zeros_like(o_ref)

    def step(i, carry):
        i0 = pl.multiple_of(i * SUBLANES, SUBLANES)
        y8 = y_ref[pl.ds(i0, SUBLANES), :]
        tok = [idx_ref[base + i0 + k] for k in range(SUBLANES)]
        acc = [o_ref[pl.ds(tok[k], 1), :] for k in range(SUBLANES)]
        for k in range(SUBLANES):
            o_ref[pl.ds(tok[k], 1), :] = acc[k] + y8[k:k + 1, :]
        return carry

    lax.fori_loop(0, cap // SUBLANES, step, 0)

    @pl.when(e == n_exp - 1)
    def _():
        o_ref[...] = x_ref[...] + gate_ref[...] * o_ref[...]


def moe_combine(geo, idx, y, xc, modp, layer, n, row_off, slot_off, mod_ctx):
    S, E, cap = idx.shape
    D = geo.D
    dc = _pick(D, (1024, 512, 256, 128))
    rb, sb = row_off // n, slot_off // cap
    mod_row = (lambda s: geo.B) if mod_ctx else (lambda s: s)
    once = pl.Buffered(1)
    grid_spec = pltpu.PrefetchScalarGridSpec(
        num_scalar_prefetch=1,
        grid=(S, D // dc, E),
        in_specs=[pl.BlockSpec((None, cap, dc), lambda s, c, e, idx: (e, sb + s, c)),
                  pl.BlockSpec((n, dc), lambda s, c, e, idx: (rb + s, c), pipeline_mode=once),
                  pl.BlockSpec((None, None, None, 1, dc),
                               lambda s, c, e, idx: (layer, mod_row(s), 5, 0, c))],
        out_specs=pl.BlockSpec((n, dc), lambda s, c, e, idx: (rb + s, c), pipeline_mode=once),
    )
    return pl.pallas_call(
        functools.partial(_combine_body, cap=cap, n_exp=E),
        out_shape=jax.ShapeDtypeStruct(xc.shape, xc.dtype),
        grid_spec=grid_spec,
        input_output_aliases={2: 0},
        compiler_params=_cparams("parallel", "parallel", "arbitrary"),
        name="moe_combine",
    )(idx.reshape(-1), y, xc, modp)


def moe_layer(geo, xc, modp, layer, gain, router, w_gate, w_up, w_down, with_ctx):
    B, L, Lc = geo.B, geo.L, geo.Lc
    E = router.shape[1]
    hf, aff = moe_router(geo, xc, modp, layer, gain, router, with_ctx)
    groups = [(L, 0)] + ([(Lc, geo.RL)] if with_ctx else [])
    sel = [moe_select(aff, B, n, off) for n, off in groups]
    slot_offs = [0, B * sel[0][0].shape[2]]
    rows = sum(B * s[0].shape[2] for s in sel)
    cap_l = sel[0][0].shape[2]
    rows_alloc = -(-rows // cap_l) * cap_l
    xg = None
    for (n, off), (idx, _), so in zip(groups, sel, slot_offs):
        xg = moe_gather(idx, hf, xg, rows_alloc, n, off, so)
    g_col = jnp.concatenate([g.transpose(1, 0, 2).reshape(E, -1) for _, g in sel]
                            + [jnp.zeros((E, rows_alloc - rows), F32)], axis=1)[:, :, None]
    y = moe_experts(xg, g_col, rows, layer, w_gate, w_up, w_down)
    for k, ((n, off), (idx, _), so) in enumerate(zip(groups, sel, slot_offs)):
        xc = moe_combine(geo, idx, y, xc, modp, layer, n, off, so, mod_ctx=k == 1)
    return xc


def _final_norm_body(x_ref, g_ref, o_ref):
    x = x_ref[...]
    o_ref[...] = x * lax.rsqrt(jnp.mean(x * x, axis=-1, keepdims=True) + EPS) * g_ref[...]


def final_norm(geo, xc, gain):
    D, tm = geo.D, geo.tm
    return pl.pallas_call(
        _final_norm_body,
        out_shape=jax.ShapeDtypeStruct((geo.RL, D), F32),
        grid=(geo.RL // tm,),
        in_specs=[pl.BlockSpec((tm, D), lambda i: (i, 0)), pl.BlockSpec((1, D), lambda i: (0, 0))],
        out_specs=pl.BlockSpec((tm, D), lambda i: (i, 0)),
        compiler_params=_cparams("parallel"),
        name="final_norm",
    )(xc, gain.reshape(1, D))


def kernel(x, c, ctx, c_ctx, ada_w, ada_b, norm_mix_g, norm_ffn_g, final_norm_g, conv_w_in, conv_w, conv_w_out, ssm_w_in, ssm_a_re, ssm_a_im, ssm_log_dt, ssm_b_re, ssm_b_im, ssm_c_re, ssm_c_im, ssm_d, ssm_w_glu, attn_w_qkv, attn_lambda, attn_subln_g, attn_w_o, moe_router, moe_w_gate, moe_w_up, moe_w_down):
    B, L, D = x.shape
    Lc = ctx.shape[1]
    depth = ada_w.shape[0]
    geo = Geo(B, L, Lc, D)
    xc = jnp.concatenate([x.reshape(B * L, D), ctx.reshape(B * Lc, D),
                          jnp.zeros((geo.R - geo.Rv, D), F32)], axis=0)
    c_all = jnp.zeros((SUBLANES, D), F32).at[:B].set(c).at[B].set(c_ctx)
    modp = ada_modulation(c_all, ada_w, ada_b).reshape(depth, SUBLANES, 6, 1, D)
    ctx_live = [any(MIXER_READS_CTX[j % N_MIXERS] for j in range(i + 1, depth)) for i in range(depth)]
    for i in range(depth):
        kind, slot = i % N_MIXERS, i // N_MIXERS
        ctx_out = ctx_live[i]
        if kind == 0:
            xc = conv_mixer(geo, xc, modp, i, norm_mix_g[i], conv_w_in[slot], conv_w[slot],
                            conv_w_out[slot], ctx_out)
        elif kind == 1:
            xc = s5_mixer(geo, xc, modp, i, norm_mix_g[i], ssm_w_in[slot], ssm_a_re[slot], ssm_a_im[slot],
                          ssm_log_dt[slot], ssm_b_re[slot], ssm_b_im[slot], ssm_c_re[slot], ssm_c_im[slot],
                          ssm_d[slot], ssm_w_glu[slot])
        else:
            xc = attn_mixer(geo, xc, modp, i, norm_mix_g[i], attn_w_qkv[slot], attn_lambda[slot],
                            attn_subln_g[slot], attn_w_o[slot])
        xc = moe_layer(geo, xc, modp, i, norm_ffn_g[i], moe_router[i], moe_w_gate, moe_w_up, moe_w_down,
                       ctx_out)
    return final_norm(geo, xc, final_norm_g).reshape(B, L, D)
```

```python
import functools
import math

import jax
import jax.numpy as jnp
from jax import lax
from jax.experimental import pallas as pl
from jax.experimental.pallas import tpu as pltpu

F32 = jnp.float32
BF16 = jnp.bfloat16
EPS = 1e-6
N_MIXERS = 3
MIXER_READS_CTX = (False, True, True)
GRID_W = 64
ROPE_THETA = 10000.0
CAPACITY_FACTOR = 2
SSM_CHUNK = 16
SUBLANES = 8
CONV_HALO = 16
ATTN_SUBTILES = 8
LANES = 128
VMEM_LIMIT = 56 * 1024 * 1024


def _cparams(*sem):
    return pltpu.CompilerParams(dimension_semantics=sem, vmem_limit_bytes=VMEM_LIMIT)


def _pick(n, cands):
    for c in cands:
        if n % c == 0:
            return c
    raise ValueError(f"no tile in {cands} divides {n}")


class Geo:
    def __init__(self, B, L, Lc, D, tm=None, tn=None):
        self.B, self.L, self.Lc, self.D = B, L, Lc, D
        self.RL = B * L
        self.Rv = B * (L + Lc)
        self.R = -(-self.Rv // L) * L
        self.tm = tm or _pick(math.gcd(L, B * Lc), (512, 256, 128, 64, 32, 16, 8))
        assert L % self.tm == 0 and (B * Lc) % self.tm == 0
        self.tn = tn or _pick(D, (1024, 512, 256, 128))

    def whole_width(self):
        return Geo(self.B, self.L, self.Lc, self.D, tm=self.tm, tn=self.D)

    def mod_row(self, i):
        return jnp.minimum((i * self.tm) // self.L, self.B)

    def tiles(self, with_ctx):
        return (self.Rv if with_ctx else self.RL) // self.tm


def _ada_body(c_ref, w_ref, b_ref, o_ref):
    c = c_ref[...]
    s = (c * jax.nn.sigmoid(c)).astype(BF16)
    o_ref[...] = jnp.dot(s, w_ref[...].astype(BF16), preferred_element_type=F32) + b_ref[...]


def ada_modulation(c_all, ada_w, ada_b):
    depth, D, N = ada_w.shape
    tn = _pick(N, (1024, 512, 256, 128))
    return pl.pallas_call(
        _ada_body,
        out_shape=jax.ShapeDtypeStruct((depth, SUBLANES, N), F32),
        grid=(depth, N // tn),
        in_specs=[pl.BlockSpec((SUBLANES, D), lambda l, j: (0, 0)),
                  pl.BlockSpec((None, D, tn), lambda l, j: (l, 0, j)),
                  pl.BlockSpec((None, 1, tn), lambda l, j: (l, 0, j))],
        out_specs=pl.BlockSpec((None, SUBLANES, tn), lambda l, j: (l, 0, j)),
        compiler_params=_cparams("parallel", "parallel"),
        name="ada_modulation",
    )(c_all, ada_w, ada_b.reshape(depth, 1, N))


def _mm_body(*refs, n_lhs, n_w, n_epi, prologue, epilogue):
    lhs = refs[:n_lhs]
    ws = refs[n_lhs:n_lhs + n_w]
    epi = refs[n_lhs + n_w:n_lhs + n_w + n_epi]
    outs = refs[n_lhs + n_w + n_epi:-1]
    hb = refs[-1]

    @pl.when(pl.program_id(1) == 0)
    def _():
        hb[...] = prologue(*lhs).astype(hb.dtype)

    h = hb[...]
    accs = [jnp.dot(h, w[...], preferred_element_type=F32) for w in ws]
    for o, v in zip(outs, epilogue(accs, *epi)):
        o[...] = v.astype(o.dtype).reshape(o.shape)


def fused_matmul(name, geo, n_tiles, lhs, w, n_split, epi, prologue, epilogue, outs, alias=None,
                 out_layout=None):
    K = w.shape[0]
    N = w.shape[1] // n_split
    tm, tn = geo.tm, geo.tn
    nj = N // tn
    w_specs = [pl.BlockSpec((K, tn), functools.partial(lambda i, j, s: (0, s * nj + j), s=s))
               for s in range(n_split)]
    arrays = [a for a, _ in lhs] + [w] * n_split + [a for a, _ in epi]
    specs = [s for _, s in lhs] + w_specs + [s for _, s in epi]
    body = functools.partial(_mm_body, n_lhs=len(lhs), n_w=n_split, n_epi=len(epi),
                             prologue=prologue, epilogue=epilogue)
    out_shape, out_spec = out_layout or ((geo.R, N), pl.BlockSpec((tm, tn), lambda i, j: (i, j)))
    res = pl.pallas_call(
        body,
        out_shape=[jax.ShapeDtypeStruct(out_shape, dt) for dt in outs],
        grid=(n_tiles, nj),
        in_specs=specs,
        out_specs=[out_spec for _ in outs],
        scratch_shapes=[pltpu.VMEM((tm, K), BF16)],
        input_output_aliases=alias or {},
        compiler_params=_cparams("parallel", "arbitrary"),
        name=name,
    )(*arrays)
    return res


def _row_full(geo, K):
    return pl.BlockSpec((geo.tm, K), lambda i, j: (i, 0))


def _row_tile(geo):
    return pl.BlockSpec((geo.tm, geo.tn), lambda i, j: (i, j))


def _vec_full(K):
    return pl.BlockSpec((1, K), lambda i, j: (0, 0))


def _mod_full(geo, layer, k):
    return pl.BlockSpec((None, None, None, 1, geo.D),
                        lambda i, j: (layer, geo.mod_row(i), k, 0, 0))


def _mod_tile(geo, layer, k):
    return pl.BlockSpec((None, None, None, 1, geo.tn),
                        lambda i, j: (layer, geo.mod_row(i), k, 0, j))


def _norm_mod(x_ref, g_ref, sh_ref, sc_ref):
    x = x_ref[...]
    y = x * lax.rsqrt(jnp.mean(x * x, axis=-1, keepdims=True) + EPS)
    return (y * g_ref[...]) * (1.0 + sc_ref[...]) + sh_ref[...]


def _norm_mod_inputs(geo, xc, gain, modp, layer, k_shift):
    return [(xc, _row_full(geo, geo.D)), (gain.reshape(1, geo.D), _vec_full(geo.D)),
            (modp, _mod_full(geo, layer, k_shift)), (modp, _mod_full(geo, layer, k_shift + 1))]


def _residual_epilogue(accs, x_ref, gate_ref):
    return [x_ref[...] + gate_ref[...] * accs[0]]


def _conv_in_epilogue(accs):
    gate_b, gate_c, v = accs
    return [gate_b, gate_c * v]


def _conv_prologue(gb_ref, z_ref, zp_ref, zn_ref, wc_ref, *, geo):
    tm = geo.tm
    z = z_ref[...].astype(F32)
    t = lax.broadcasted_iota(jnp.int32, (tm, 1), 0)
    row = pl.program_id(0) * tm + t
    is_lat = row < geo.RL
    pos = jnp.where(is_lat, row % geo.L, (row - geo.RL) % geo.Lc)
    first = pos == 0
    last = pos == jnp.where(is_lat, geo.L - 1, geo.Lc - 1)
    halo_last = zp_ref[CONV_HALO - 1:CONV_HALO, :].astype(F32)
    z_prev = jnp.where(t == 0, halo_last, pltpu.roll(z, 1, axis=0))
    z_prev = jnp.where(first, 0.0, z_prev)
    z_next = jnp.where(t == tm - 1, zn_ref[0:1, :].astype(F32), pltpu.roll(z, tm - 1, axis=0))
    z_next = jnp.where(last, 0.0, z_next)
    wc = wc_ref[...]
    conv = wc[0:1, :] * z_prev + wc[1:2, :] * z + wc[2:3, :] * z_next
    return gb_ref[...].astype(F32) * conv


def conv_mixer(geo, xc, modp, layer, gain, w_in, w_conv, w_out, with_ctx):
    D, tm = geo.D, geo.tm
    n_tiles = geo.tiles(with_ctx)
    gb, z = fused_matmul("conv_in", geo, n_tiles,
                         _norm_mod_inputs(geo, xc, gain, modp, layer, 0),
                         w_in.astype(BF16), 3, [], _norm_mod, _conv_in_epilogue, [BF16, BF16])
    r8 = tm // CONV_HALO
    last_blk = geo.R // CONV_HALO - 1
    halo_prev = pl.BlockSpec((CONV_HALO, D), lambda i, j: (jnp.maximum(i * r8 - 1, 0), 0))
    halo_next = pl.BlockSpec((CONV_HALO, D), lambda i, j: (jnp.minimum((i + 1) * r8, last_blk), 0))
    lhs = [(gb, _row_full(geo, D)), (z, _row_full(geo, D)), (z, halo_prev), (z, halo_next),
           (w_conv, pl.BlockSpec((3, D), lambda i, j: (0, 0)))]
    gw = geo.whole_width()
    epi = [(xc, _row_tile(gw)), (modp, _mod_tile(gw, layer, 2))]
    (out,) = fused_matmul("conv_out", gw, n_tiles, lhs, w_out.astype(BF16), 1, epi,
                          functools.partial(_conv_prologue, geo=geo), _residual_epilogue, [F32],
                          alias={len(lhs) + 1: 0})
    return out


def _s5_operators(a_re, a_im, log_dt, b_re, b_im, c_re, c_im, T):
    G, P = a_re.shape[1:]
    Cg = b_re.shape[-1]
    k = jnp.arange(T + 1, dtype=F32)[:, None, None]
    mt, bst, cst, aT = [], [], [], []
    for d in range(2):
        dt = jnp.exp(log_dt[d])[:, None]
        mag = jnp.exp(k * dt * a_re[d])
        pw_r, pw_i = mag * jnp.cos(k * dt * a_im[d]), mag * jnp.sin(k * dt * a_im[d])
        den = a_re[d] * a_re[d] + a_im[d] * a_im[d]
        nr = (pw_r[1] - 1.0) * a_re[d] + pw_i[1] * a_im[d]
        ni = pw_i[1] * a_re[d] - (pw_r[1] - 1.0) * a_im[d]
        fr, fi = (nr / den)[..., None], (ni / den)[..., None]
        bb_r = fr * b_re[d] - fi * b_im[d]
        bb_i = fr * b_im[d] + fi * b_re[d]
        lags = slice(0, T) if d == 0 else slice(T - 1, None, -1)
        pg_r, pg_i = (v[lags].transpose(1, 0, 2)[:, :, None, :] for v in (pw_r, pw_i))
        w_r = (c_re[d][:, None] * pg_r - c_im[d][:, None] * pg_i).reshape(G, T * Cg, P)
        w_i = (c_re[d][:, None] * pg_i + c_im[d][:, None] * pg_r).reshape(G, T * Cg, P)
        kk = (jnp.einsum('gmp,gpc->gcm', w_r, bb_r, precision='high')
              - jnp.einsum('gmp,gpc->gcm', w_i, bb_i, precision='high'))
        mt.append(kk.reshape(G * Cg, T * Cg))
        e = jnp.arange(T - 1, -1, -1) if d == 0 else jnp.arange(T)
        er, ei = pw_r[e], pw_i[e]
        s_r = er[:, :, :, None] * bb_r[None] - ei[:, :, :, None] * bb_i[None]
        s_i = er[:, :, :, None] * bb_i[None] + ei[:, :, :, None] * bb_r[None]
        bst.append(tuple(v.transpose(1, 0, 3, 2).reshape(G, T * Cg, P) for v in (s_r, s_i)))
        e = jnp.arange(1, T + 1) if d == 0 else jnp.arange(T, 0, -1)
        er, ei = pw_r[e], pw_i[e]
        o_r = c_re[d][None] * er[:, :, None, :] - c_im[d][None] * ei[:, :, None, :]
        o_i = -(c_re[d][None] * ei[:, :, None, :] + c_im[d][None] * er[:, :, None, :])
        cst.append(tuple(v.transpose(1, 3, 0, 2).reshape(G, P, T * Cg) for v in (o_r, o_i)))
        aT.append((pw_r[T], pw_i[T]))
    return mt, bst, cst, aT


def _lane_tile_operators(bst, cst, T, gl):
    G, TC, P = bst[0][0].shape
    Cg = TC // T
    GB = G // gl
    a_in = jnp.concatenate([m for d in range(2) for m in bst[d]], axis=2)
    a_in = a_in.reshape(GB, gl, T, Cg, 4 * P).transpose(0, 2, 1, 3, 4).reshape(GB, T * gl * Cg, 4 * P)
    a_read = jnp.stack([m.reshape(GB, gl * P, TC) for d in range(2) for m in cst[d]], axis=1)
    return a_in.astype(BF16), a_read.reshape(GB, 4 * gl * P, TC).astype(BF16)


def _toeplitz_rows(kf_ref, kr_ref, a_ref, *, T, Cg):
    n, TC = kf_ref.shape
    kf, kr = kf_ref[...], kr_ref[...]
    lane = lax.broadcasted_iota(jnp.int32, (1, TC), 1)
    for x in range(T):
        right, left = x * Cg, (T - 1 - x) * Cg
        fwd = jnp.where(lane >= right, kf if right == 0 else pltpu.roll(kf, right, axis=1), 0.0)
        bwd = jnp.where(lane < TC - left, kr if left == 0 else pltpu.roll(kr, TC - left, axis=1), 0.0)
        a_ref[x * n:(x + 1) * n, :] = (fwd + bwd).astype(a_ref.dtype)


def _spread_block_diag(a_ref, w_ref, row0, *, gl, row_blk, col_blk):
    n_rows, k = a_ref.shape
    n_cols = k * gl
    kr = lax.broadcasted_iota(jnp.int32, (k, n_cols), 0)
    kc = lax.broadcasted_iota(jnp.int32, (k, n_cols), 1)
    rep = jnp.where((kr // col_blk == kc // (gl * col_blk)) & (kr % col_blk == kc % col_blk),
                    1.0, 0.0).astype(a_ref.dtype)
    col_group = (lax.broadcasted_iota(jnp.int32, (1, n_cols), 1) // col_blk) % gl
    step = 512 if n_rows % 512 == 0 else n_rows
    for r0 in range(0, n_rows, step):
        row_group = ((lax.broadcasted_iota(jnp.int32, (step, 1), 0) + r0) // row_blk) % gl
        full = jnp.dot(a_ref[r0:r0 + step, :], rep, preferred_element_type=F32)
        w_ref[row0 + r0:row0 + r0 + step, :] = jnp.where(row_group == col_group, full, 0.0).astype(w_ref.dtype)


def _chunk_rows(u_ref, T):
    n = u_ref.shape[0] // T
    return jnp.concatenate([u_ref[pl.ds(t, n, stride=T), :] for t in range(T)], axis=1)


def _s5_state_in_body(u_ref, a_ref, ds_ref, w_ref, *, T, gl, Cg, P):
    @pl.when(pl.program_id(1) == 0)
    def _():
        _spread_block_diag(a_ref, w_ref, 0, gl=gl, row_blk=Cg, col_blk=P)

    u = _chunk_rows(u_ref, T).astype(BF16)
    res = jnp.dot(u, w_ref[...], preferred_element_type=F32)
    w = res.shape[1] // 4
    for k in range(4):
        ds_ref[k // 2, k % 2] = res[:, k * w:(k + 1) * w]


def _s5_scan_body(ds_ref, a_ref, s_ref, *, B, nc_lat, nc_ctx):
    a_re, a_im = a_ref[0], a_ref[1]
    zero = jnp.zeros((B, ds_ref.shape[-1]), F32)

    def segment(first, n, reverse, state):
        def step(i, st):
            s_re, s_im = st
            k = first + ((n - 1 - i) if reverse else i)
            for b in range(B):
                s_ref[0, pl.ds(k + b * n, 1), :] = s_re[b:b + 1].astype(s_ref.dtype)
                s_ref[1, pl.ds(k + b * n, 1), :] = s_im[b:b + 1].astype(s_ref.dtype)
            d_re = jnp.concatenate([ds_ref[0, pl.ds(k + b * n, 1), :] for b in range(B)], axis=0)
            d_im = jnp.concatenate([ds_ref[1, pl.ds(k + b * n, 1), :] for b in range(B)], axis=0)
            return a_re * s_re - a_im * s_im + d_re, a_re * s_im + a_im * s_re + d_im
        return lax.fori_loop(0, n, step, state)

    for d in range(2):
        @pl.when(pl.program_id(0) == d)
        def _():
            st = segment(B * nc_lat, nc_ctx, d == 1, (zero, zero))
            segment(0, nc_lat, d == 1, st)


def _s5_out_body(u_ref, s_ref, kf_ref, kr_ref, ar_ref, d_ref, y_ref, w_ref, ai_s, *, T, gl, Cg, P):
    @pl.when(pl.program_id(1) == 0)
    def _():
        _toeplitz_rows(kf_ref, kr_ref, ai_s, T=T, Cg=Cg)
        _spread_block_diag(ai_s, w_ref, 0, gl=gl, row_blk=Cg, col_blk=Cg)
        _spread_block_diag(ar_ref, w_ref, ai_s.shape[0], gl=gl, row_blk=P, col_blk=Cg)

    u = _chunk_rows(u_ref, T)
    s = jnp.concatenate([s_ref[d, part] for d in range(2) for part in range(2)], axis=1)
    y = (jnp.dot(jnp.concatenate([u, s], axis=1).astype(BF16), w_ref[...], preferred_element_type=F32)
         + u * d_ref[...])
    z = jax.nn.gelu(y)
    n = z.shape[0]
    for t in range(T):
        y_ref[pl.ds(t, n, stride=T), :] = z[:, t * LANES:(t + 1) * LANES]


def _glu_epilogue(accs, x_ref, gate_ref):
    val, gate = accs
    return [x_ref[...] + gate_ref[...] * (val * jax.nn.sigmoid(gate))]


def _identity_prologue(z_ref):
    return z_ref[...]


def s5_mixer(geo, xc, modp, layer, gain, w_in, a_re, a_im, log_dt, b_re, b_im, c_re, c_im, d_skip, w_glu):
    B, L, Lc, D = geo.B, geo.L, geo.Lc, geo.D
    G, P = a_re.shape[1:]
    Cg, T = D // G, SSM_CHUNK
    gl = LANES // Cg
    GB = G // gl
    nc_ctx, nc_lat = Lc // T, L // T
    rows = geo.Rv // T
    rt = _pick(rows, (272, 256, 128, 64, 32, 16))
    (u,) = fused_matmul("s5_in", geo.whole_width(), geo.tiles(True),
                        _norm_mod_inputs(geo, xc, gain, modp, layer, 0),
                        w_in.astype(BF16), 1, [], _norm_mod, lambda accs: accs, [F32])
    u_specs = [pl.BlockSpec((rt * T, LANES), lambda gb, r: (r, gb))]

    kk, bst, cst, aT = _s5_operators(a_re, a_im, log_dt, b_re, b_im, c_re, c_im, T)
    a_in, a_read = _lane_tile_operators(bst, cst, T, gl)
    kf, kr = (m.reshape(GB, gl * Cg, T * Cg) for m in kk)
    a_tab = jnp.stack([jnp.stack([v.reshape(1, G * P) for v in aT[d]]) for d in range(2)])
    d_lane = jnp.tile(d_skip.reshape(GB, 1, LANES), (1, 1, T))

    KU, KS = T * LANES, 4 * gl * P
    dims = dict(T=T, gl=gl, Cg=Cg, P=P)
    ds = pl.pallas_call(
        functools.partial(_s5_state_in_body, **dims),
        out_shape=jax.ShapeDtypeStruct((2, 2, rows, G * P), F32),
        grid=(GB, rows // rt),
        in_specs=u_specs + [pl.BlockSpec((None, KU, 4 * P), lambda gb, r: (gb, 0, 0))],
        out_specs=pl.BlockSpec((2, 2, rt, gl * P), lambda gb, r: (0, 0, r, gb)),
        scratch_shapes=[pltpu.VMEM((KU, KS), BF16)],
        compiler_params=_cparams("parallel", "arbitrary"),
        name="s5_state_in",
    )(u, a_in)

    lb = _pick(G * P, (1024, 512, 256, 128))
    s_start = pl.pallas_call(
        functools.partial(_s5_scan_body, B=B, nc_lat=nc_lat, nc_ctx=nc_ctx),
        out_shape=jax.ShapeDtypeStruct((2, 2, rows, G * P), F32),
        grid=(2, G * P // lb),
        in_specs=[pl.BlockSpec((None, 2, rows, lb), lambda d, j: (d, 0, 0, j)),
                  pl.BlockSpec((None, 2, 1, lb), lambda d, j: (d, 0, 0, j))],
        out_specs=pl.BlockSpec((None, 2, rows, lb), lambda d, j: (d, 0, 0, j)),
        compiler_params=_cparams("parallel", "parallel"),
        name="s5_scan",
    )(ds, a_tab)

    lag_spec = pl.BlockSpec((None, gl * Cg, T * Cg), lambda gb, r: (gb, 0, 0))
    y = pl.pallas_call(
        functools.partial(_s5_out_body, **dims),
        out_shape=jax.ShapeDtypeStruct((geo.R, D), F32),
        grid=(GB, rows // rt),
        in_specs=u_specs + [pl.BlockSpec((2, 2, rt, gl * P), lambda gb, r: (0, 0, r, gb)),
                            lag_spec, lag_spec,
                            pl.BlockSpec((None, KS, T * Cg), lambda gb, r: (gb, 0, 0)),
                            pl.BlockSpec((None, 1, KU), lambda gb, r: (gb, 0, 0))],
        out_specs=pl.BlockSpec((rt * T, LANES), lambda gb, r: (r, gb)),
        scratch_shapes=[pltpu.VMEM((KU + KS, KU), BF16), pltpu.VMEM((KU, T * Cg), BF16)],
        compiler_params=_cparams("parallel", "arbitrary"),
        name="s5_out",
    )(u, s_start, kf, kr, a_read, d_lane)

    lhs = [(y, _row_full(geo, D))]
    epi = [(xc, _row_tile(geo)), (modp, _mod_tile(geo, layer, 2))]
    (out,) = fused_matmul("s5_glu", geo, geo.tiles(True), lhs, w_glu.astype(BF16), 2, epi,
                          _identity_prologue, _glu_epilogue, [F32], alias={len(lhs) + 2: 0})
    return out


def _rope_tables(geo, head_dim):
    n_freq = head_dim // 4
    pos = jnp.arange(geo.L)
    inv_freq = ROPE_THETA ** (-jnp.arange(n_freq, dtype=F32) / n_freq)
    d = jnp.arange(LANES) % head_dim
    axis, half, f = d // (2 * n_freq), (d % (2 * n_freq)) // n_freq, d % n_freq
    p = jnp.where(axis[None, :] == 0, (pos // GRID_W)[:, None], (pos % GRID_W)[:, None]).astype(F32)
    ang = p * inv_freq[f][None, :]
    cos = jnp.cos(ang)
    sin = jnp.sin(ang) * jnp.where(half == 0, -1.0, 1.0)[None, :]
    rest = geo.R - geo.RL
    return (jnp.concatenate([jnp.tile(cos, (geo.B, 1)), jnp.ones((rest, LANES), F32)], axis=0),
            jnp.concatenate([jnp.tile(sin, (geo.B, 1)), jnp.zeros((rest, LANES), F32)], axis=0))


def _qkv_epilogue(accs, cos_ref, sin_ref, *, scale, n_freq):
    q, k, v = accs
    reps = q.shape[1] // LANES
    cos = jnp.concatenate([cos_ref[...]] * reps, axis=1)
    sin = jnp.concatenate([sin_ref[...]] * reps, axis=1)
    tn = q.shape[1]
    lane = lax.broadcasted_iota(jnp.int32, (1, tn), 1)
    first_half = (lane % (2 * n_freq)) < n_freq

    def rope(x):
        partner = jnp.where(first_half, pltpu.roll(x, tn - n_freq, axis=1), pltpu.roll(x, n_freq, axis=1))
        return x * cos + partner * sin

    return [rope(q) * scale, rope(k), v]


def _attn_body(q_ref, kl_ref, kc_ref, vl_ref, vc_ref, lam_ref, g_ref, o_ref, k_all, v_aug, *, lambda_init):
    q = q_ref[...]
    hw = q.shape[1]
    head_dim = hw // 2
    L = kl_ref.shape[0]
    lane = lax.broadcasted_iota(jnp.int32, (1, hw), 1)
    nt = (((1,), (1,)), ((), ()))

    @pl.when(pl.program_id(2) == 0)
    def _():
        k_all[:L, :] = kl_ref[...]
        k_all[L:, :] = kc_ref[...]
        v_aug[:L, :hw] = vl_ref[...]
        v_aug[L:, :hw] = vc_ref[...]
        v_aug[:, hw:] = jnp.ones((v_aug.shape[0], hw), v_aug.dtype)

    def attend(qm):
        s = lax.dot_general(qm, k_all[...], nt, preferred_element_type=F32)
        p = jnp.exp2(s - jnp.max(s, axis=1, keepdims=True)).astype(BF16)
        o = jnp.dot(p, v_aug[...], preferred_element_type=F32)
        return o[:, :hw] / o[:, hw:hw + 1]

    lp = lam_ref[...]
    lam = (jnp.exp(jnp.sum(lp[0:1] * lp[1:2], axis=1, keepdims=True))
           - jnp.exp(jnp.sum(lp[2:3] * lp[3:4], axis=1, keepdims=True)) + lambda_init)
    rows = q.shape[0] // ATTN_SUBTILES
    for r in range(ATTN_SUBTILES):
        qr = q[r * rows:(r + 1) * rows]
        o = (attend(jnp.where(lane < head_dim, qr, jnp.zeros_like(qr)))
             - lam * attend(jnp.where(lane >= head_dim, qr, jnp.zeros_like(qr))))
        o = o * lax.rsqrt(jnp.mean(o * o, axis=-1, keepdims=True) + EPS) * g_ref[...]
        o_ref[r * rows:(r + 1) * rows, :] = (o * (1.0 - lambda_init)).astype(o_ref.dtype)


def attn_mixer(geo, xc, modp, layer, gain, w_qkv, lam_params, subln_g, w_o):
    B, L, Lc, D = geo.B, geo.L, geo.Lc, geo.D
    head_dim = lam_params.shape[1]
    hw = 2 * head_dim
    assert hw == LANES and geo.R % Lc == 0
    H = D // hw
    lambda_init = 0.8 - 0.6 * math.exp(-0.3 * layer)
    cos, sin = _rope_tables(geo, head_dim)
    tab = pl.BlockSpec((geo.tm, LANES), lambda i, j: (i, 0))
    q, k, v = fused_matmul(
        "attn_qkv", geo, geo.tiles(True), _norm_mod_inputs(geo, xc, gain, modp, layer, 0),
        w_qkv.astype(BF16), 3, [(cos, tab), (sin, tab)], _norm_mod,
        functools.partial(_qkv_epilogue, scale=head_dim ** -0.5 * math.log2(math.e),
                          n_freq=head_dim // 4), [BF16] * 3)
    tq = _pick(L, (2048, 1024, 512, 256, 128))
    nq = L // tq
    ctx_blk = geo.RL // Lc
    lat_kv = pl.BlockSpec((L, hw), lambda b, h, i: (b, h))
    ctx_kv = pl.BlockSpec((Lc, hw), lambda b, h, i: (ctx_blk + b, h))
    o = pl.pallas_call(
        functools.partial(_attn_body, lambda_init=lambda_init),
        out_shape=jax.ShapeDtypeStruct((geo.R, D), BF16),
        grid=(B, H, nq),
        in_specs=[pl.BlockSpec((tq, hw), lambda b, h, i: (b * nq + i, h)),
                  lat_kv, ctx_kv, lat_kv, ctx_kv,
                  pl.BlockSpec((4, head_dim), lambda b, h, i: (0, 0)),
                  pl.BlockSpec((1, hw), lambda b, h, i: (0, 0))],
        out_specs=pl.BlockSpec((tq, hw), lambda b, h, i: (b * nq + i, h)),
        scratch_shapes=[pltpu.VMEM((L + Lc, hw), BF16), pltpu.VMEM((L + Lc, 2 * hw), BF16)],
        compiler_params=_cparams("parallel", "parallel", "arbitrary"),
        name="diff_attention",
    )(q, k, k, v, v, lam_params, subln_g.reshape(1, hw))
    lhs = [(o, _row_full(geo, D))]
    gw = geo.whole_width()
    epi = [(xc, _row_tile(gw)), (modp, _mod_tile(gw, layer, 2))]
    (out,) = fused_matmul("attn_out", gw, geo.tiles(False), lhs, w_o.astype(BF16), 1, epi,
                          _identity_prologue, _residual_epilogue, [F32], alias={len(lhs) + 1: 0})
    return out


def _router_body(x_ref, g_ref, sh_ref, sc_ref, rt_ref, hf_ref, aff_ref):
    h = _norm_mod(x_ref, g_ref, sh_ref, sc_ref)
    half = h.shape[1] // 2
    hb = h.astype(BF16)
    hbf = hb.astype(F32)
    hi = pltpu.bitcast(hbf[:, :half], jnp.uint32)
    lo = lax.shift_right_logical(pltpu.bitcast(hbf[:, half:], jnp.uint32), jnp.uint32(16))
    hf_ref[...] = hi | lo
    h_lo = (h - hbf).astype(BF16)
    rt = rt_ref[...]
    r_hi = rt.astype(BF16)
    r_lo = (rt - r_hi.astype(F32)).astype(BF16)
    nt = (((1,), (1,)), ((), ()))
    logits = (lax.dot_general(r_hi, hb, nt, preferred_element_type=F32)
              + lax.dot_general(r_hi, h_lo, nt, preferred_element_type=F32)
              + lax.dot_general(r_lo, hb, nt, preferred_element_type=F32))
    m = jnp.max(logits, axis=0, keepdims=True)
    e = jnp.exp(logits - m)
    aff_ref[...] = e / jnp.sum(e, axis=0, keepdims=True)


def moe_router(geo, xc, modp, layer, gain, router, with_ctx):
    D, tm = geo.D, geo.tm
    E = router.shape[1]
    i_only = lambda f: (lambda i: f(i, 0))
    specs = [s for _, s in _norm_mod_inputs(geo, xc, gain, modp, layer, 3)]
    specs = [pl.BlockSpec(s.block_shape, i_only(s.index_map)) for s in specs]
    return pl.pallas_call(
        _router_body,
        out_shape=[jax.ShapeDtypeStruct((geo.R, D // 2), jnp.uint32),
                   jax.ShapeDtypeStruct((E, geo.R), F32)],
        grid=(geo.tiles(with_ctx),),
        in_specs=specs + [pl.BlockSpec((E, D), lambda i: (0, 0))],
        out_specs=[pl.BlockSpec((tm, D // 2), lambda i: (i, 0)),
                   pl.BlockSpec((E, tm), lambda i: (0, i))],
        compiler_params=_cparams("parallel"),
        name="moe_router",
    )(xc, gain.reshape(1, D), modp, modp, router.T)


def _prefix_count(x, tri):
    n = x.shape[1]
    off = jnp.zeros((x.shape[0], 1), F32)
    parts = []
    for j in range(n // LANES):
        blk = x[:, j * LANES:(j + 1) * LANES]
        inc = jnp.dot(blk.astype(BF16), tri, preferred_element_type=F32)
        parts.append(inc - blk + off)
        off = off + inc[:, LANES - 1:LANES]
    return jnp.concatenate(parts, axis=1)


def _select_body(aff_ref, idx_ref, g_ref, *, cap):
    a = aff_ref[...]
    E, n = a.shape
    bits = pltpu.bitcast(a, jnp.int32)
    thr = jnp.zeros((E, 1), jnp.int32)
    for k in range(30, -1, -1):
        cand = thr | jnp.int32(1 << k)
        cnt = jnp.sum((bits >= cand).astype(F32), axis=1, keepdims=True)
        thr = jnp.where(cnt >= cap, cand, thr)
    gt = bits > thr
    eq = bits == thr
    need = cap - jnp.sum(gt.astype(F32), axis=1, keepdims=True)
    r = lax.broadcasted_iota(jnp.int32, (LANES, LANES), 0)
    c = lax.broadcasted_iota(jnp.int32, (LANES, LANES), 1)
    tri = (r <= c).astype(BF16)
    sel = gt | (eq & (_prefix_count(eq.astype(F32), tri) < need))
    self32 = sel.astype(F32)
    slot = _prefix_count(self32, tri)
    slot = jnp.where(sel, slot, -1.0)
    t = lax.broadcasted_iota(jnp.int32, (1, n), 1)
    t_hi = (t // 64).astype(F32)
    t_lo = (t % 64).astype(F32)
    a1 = a.astype(BF16).astype(F32)
    a2 = (a - a1).astype(BF16).astype(F32)
    a3 = (a - a1 - a2).astype(BF16).astype(F32)
    s_iota = lax.broadcasted_iota(jnp.int32, (cap, 1), 0).astype(F32)
    row = lax.broadcasted_iota(jnp.int32, (SUBLANES, 1), 0)
    nt = (((1,), (1,)), ((), ()))
    for e in range(E):
        onehot = jnp.where(slot[e:e + 1, :] == s_iota, 1.0, 0.0).astype(BF16)
        lhs = jnp.where(row == 0, t_hi, jnp.where(row == 1, t_lo, jnp.where(
            row == 2, a1[e:e + 1, :], jnp.where(row == 3, a2[e:e + 1, :], jnp.where(
                row == 4, a3[e:e + 1, :], 0.0))))).astype(BF16)
        res = lax.dot_general(lhs, onehot, nt, preferred_element_type=F32)
        idx_ref[e:e + 1, :] = (res[0:1, :] * 64.0 + res[1:2, :]).astype(jnp.int32)
        g_ref[e:e + 1, :] = res[2:3, :] + res[3:4, :] + res[4:5, :]


def moe_select(aff, n_samples, n, col_off):
    E = aff.shape[0]
    cap = max(1, CAPACITY_FACTOR * n // E)
    blk0 = col_off // n
    return pl.pallas_call(
        functools.partial(_select_body, cap=cap),
        out_shape=[jax.ShapeDtypeStruct((n_samples, E, cap), jnp.int32),
                   jax.ShapeDtypeStruct((n_samples, E, cap), F32)],
        grid=(n_samples,),
        in_specs=[pl.BlockSpec((E, n), lambda s: (0, blk0 + s))],
        out_specs=[pl.BlockSpec((None, E, cap), lambda s: (s, 0, 0)),
                   pl.BlockSpec((None, E, cap), lambda s: (s, 0, 0))],
        compiler_params=_cparams("parallel"),
        name="moe_select",
    )(aff)


def _gather_body(idx_ref, hf_ref, *rest, cap, n_exp):
    o_ref = rest[-1]
    base = (pl.program_id(0) * n_exp + pl.program_id(1)) * cap

    def step(i, carry):
        i0 = pl.multiple_of(i * SUBLANES, SUBLANES)
        rows = [hf_ref[pl.ds(idx_ref[base + i0 + k], 1), :] for k in range(SUBLANES)]
        dst = o_ref.at[pl.ds(i0, SUBLANES), :]
        for k in range(SUBLANES):
            dst[k:k + 1, :] = rows[k]
        return carry

    lax.fori_loop(0, cap // SUBLANES, step, 0)


def moe_gather(idx, hf, xg, rows_alloc, n, row_off, slot_off):
    S, E, cap = idx.shape
    W = hf.shape[1]
    rb, sb = row_off // n, slot_off // cap
    prev = [] if xg is None else [xg]
    grid_spec = pltpu.PrefetchScalarGridSpec(
        num_scalar_prefetch=1,
        grid=(S, E),
        in_specs=[pl.BlockSpec((n, W), lambda s, e, idx: (rb + s, 0))]
        + [pl.BlockSpec(memory_space=pl.ANY) for _ in prev],
        out_specs=pl.BlockSpec((None, cap, W), lambda s, e, idx: (e, sb + s, 0)),
    )
    return pl.pallas_call(
        functools.partial(_gather_body, cap=cap, n_exp=E),
        out_shape=jax.ShapeDtypeStruct((E, rows_alloc, W), hf.dtype),
        grid_spec=grid_spec,
        input_output_aliases={2: 0} if prev else {},
        compiler_params=_cparams("parallel", "arbitrary"),
        name="moe_gather",
    )(idx.reshape(-1), hf, *prev)


def _expert_body(xg_ref, g_ref, wg_ref, wu_ref, wd_ref, y_ref, wg_s, wu_s, wd_s, *, n_exp):
    j, k = pl.program_id(0), pl.program_id(1)
    dk, fk = wg_ref.shape[0], wd_ref.shape[0]

    def round_chunk(slot):
        rows_d = pl.ds(pl.multiple_of(k * dk, dk), dk)
        wg_s[slot, rows_d, :] = wg_ref[...].astype(BF16)
        wu_s[slot, rows_d, :] = wu_ref[...].astype(BF16)
        wd_s[slot, pl.ds(pl.multiple_of(k * fk, fk), fk), :] = wd_ref[...].astype(BF16)

    def ffn_tile(slot):
        u = xg_ref[...]
        half = u.shape[1]
        a = pltpu.bitcast(u & jnp.uint32(0xFFFF0000), F32).astype(BF16)
        b = pltpu.bitcast(lax.shift_left(u, jnp.uint32(16)), F32).astype(BF16)
        gate = (jnp.dot(a, wg_s[slot, :half, :], preferred_element_type=F32)
                + jnp.dot(b, wg_s[slot, half:, :], preferred_element_type=F32))
        up = (jnp.dot(a, wu_s[slot, :half, :], preferred_element_type=F32)
              + jnp.dot(b, wu_s[slot, half:, :], preferred_element_type=F32))
        act = (gate * jax.nn.sigmoid(gate) * up).astype(BF16)
        y_ref[...] = jnp.dot(act, wd_s[slot], preferred_element_type=F32) * g_ref[...]

    @pl.when(j == 0)
    def _():
        round_chunk(0)
        y_ref[...] = jnp.zeros_like(y_ref)

    for parity in range(2):
        @pl.when((j > 0) & (j < n_exp) & (j % 2 == parity))
        def _():
            ffn_tile(1 - parity)
            round_chunk(parity)

    @pl.when(j == n_exp)
    def _():
        ffn_tile((n_exp + 1) % 2)


def moe_experts(xg, g_col, rows, layer, w_gate, w_up, w_down):
    E, rows_alloc, W = xg.shape
    D, F = w_gate.shape[2], w_gate.shape[3]
    tr = max(t for t in range(16, 513, 16) if rows % t == 0)
    K = rows // tr
    assert D % (16 * K) == 0 and F % (16 * K) == 0
    w_exp = lambda j: jnp.minimum(j, E - 1)
    x_exp = lambda j: jnp.maximum(j - 1, 0)
    y_exp = lambda j: jnp.where(j == 0, E, j - 1)
    return pl.pallas_call(
        functools.partial(_expert_body, n_exp=E),
        out_shape=jax.ShapeDtypeStruct((E + 1, rows_alloc, D), F32),
        grid=(E + 1, K),
        in_specs=[pl.BlockSpec((None, tr, W), lambda j, k: (x_exp(j), k, 0)),
                  pl.BlockSpec((None, tr, 1), lambda j, k: (x_exp(j), k, 0)),
                  pl.BlockSpec((None, None, D // K, F), lambda j, k: (layer, w_exp(j), k, 0)),
                  pl.BlockSpec((None, None, D // K, F), lambda j, k: (layer, w_exp(j), k, 0)),
                  pl.BlockSpec((None, None, F // K, D), lambda j, k: (layer, w_exp(j), k, 0))],
        out_specs=pl.BlockSpec((None, tr, D), lambda j, k: (y_exp(j), k, 0)),
        scratch_shapes=[pltpu.VMEM((2, D, F), BF16), pltpu.VMEM((2, D, F), BF16),
                        pltpu.VMEM((2, F, D), BF16)],
        compiler_params=_cparams("arbitrary", "arbitrary"),
        name="moe_experts",
    )(xg, g_col, w_gate, w_up, w_down)


def _combine_body(idx_ref, y_ref, x_ref, gate_ref, o_ref, *, cap, n_exp):
    e = pl.program_id(2)
    base = (pl.program_id(0) * n_exp + e) * cap

    @pl.when(e == 0)
    def _():
        o_ref[...] = jnp.zeros_like(o_ref)

    def step(i, carry):
        i0 = pl.multiple_of(i * SUBLANES, SUBLANES)
        y8 = y_ref[pl.ds(i0, SUBLANES), :]
        tok = [idx_ref[base + i0 + k] for k in range(SUBLANES)]
        acc = [o_ref[pl.ds(tok[k], 1), :] for k in range(SUBLANES)]
        for k in range(SUBLANES):
            o_ref[pl.ds(tok[k], 1), :] = acc[k] + y8[k:k + 1, :]
        return carry

    lax.fori_loop(0, cap // SUBLANES, step, 0)

    @pl.when(e == n_exp - 1)
    def _():
        o_ref[...] = x_ref[...] + gate_ref[...] * o_ref[...]


def moe_combine(geo, idx, y, xc, modp, layer, n, row_off, slot_off, mod_ctx):
    S, E, cap = idx.shape
    D = geo.D
    dc = _pick(D, (1024, 512, 256, 128))
    rb, sb = row_off // n, slot_off // cap
    mod_row = (lambda s: geo.B) if mod_ctx else (lambda s: s)
    once = pl.Buffered(1)
    grid_spec = pltpu.PrefetchScalarGridSpec(
        num_scalar_prefetch=1,
        grid=(S, D // dc, E),
        in_specs=[pl.BlockSpec((None, cap, dc), lambda s, c, e, idx: (e, sb + s, c)),
                  pl.BlockSpec((n, dc), lambda s, c, e, idx: (rb + s, c), pipeline_mode=once),
                  pl.BlockSpec((None, None, None, 1, dc),
                               lambda s, c, e, idx: (layer, mod_row(s), 5, 0, c))],
        out_specs=pl.BlockSpec((n, dc), lambda s, c, e, idx: (rb + s, c), pipeline_mode=once),
    )
    return pl.pallas_call(
        functools.partial(_combine_body, cap=cap, n_exp=E),
        out_shape=jax.ShapeDtypeStruct(xc.shape, xc.dtype),
        grid_spec=grid_spec,
        input_output_aliases={2: 0},
        compiler_params=_cparams("parallel", "parallel", "arbitrary"),
        name="moe_combine",
    )(idx.reshape(-1), y, xc, modp)


def moe_layer(geo, xc, modp, layer, gain, router, w_gate, w_up, w_down, with_ctx):
    B, L, Lc = geo.B, geo.L, geo.Lc
    E = router.shape[1]
    hf, aff = moe_router(geo, xc, modp, layer, gain, router, with_ctx)
    groups = [(L, 0)] + ([(Lc, geo.RL)] if with_ctx else [])
    sel = [moe_select(aff, B, n, off) for n, off in groups]
    slot_offs = [0, B * sel[0][0].shape[2]]
    rows = sum(B * s[0].shape[2] for s in sel)
    cap_l = sel[0][0].shape[2]
    rows_alloc = -(-rows // cap_l) * cap_l
    xg = None
    for (n, off), (idx, _), so in zip(groups, sel, slot_offs):
        xg = moe_gather(idx, hf, xg, rows_alloc, n, off, so)
    g_col = jnp.concatenate([g.transpose(1, 0, 2).reshape(E, -1) for _, g in sel]
                            + [jnp.zeros((E, rows_alloc - rows), F32)], axis=1)[:, :, None]
    y = moe_experts(xg, g_col, rows, layer, w_gate, w_up, w_down)
    for k, ((n, off), (idx, _), so) in enumerate(zip(groups, sel, slot_offs)):
        xc = moe_combine(geo, idx, y, xc, modp, layer, n, off, so, mod_ctx=k == 1)
    return xc


def _final_norm_body(x_ref, g_ref, o_ref):
    x = x_ref[...]
    o_ref[...] = x * lax.rsqrt(jnp.mean(x * x, axis=-1, keepdims=True) + EPS) * g_ref[...]


def final_norm(geo, xc, gain):
    D, tm = geo.D, geo.tm
    return pl.pallas_call(
        _final_norm_body,
        out_shape=jax.ShapeDtypeStruct((geo.RL, D), F32),
        grid=(geo.RL // tm,),
        in_specs=[pl.BlockSpec((tm, D), lambda i: (i, 0)), pl.BlockSpec((1, D), lambda i: (0, 0))],
        out_specs=pl.BlockSpec((tm, D), lambda i: (i, 0)),
        compiler_params=_cparams("parallel"),
        name="final_norm",
    )(xc, gain.reshape(1, D))


def kernel(x, c, ctx, c_ctx, ada_w, ada_b, norm_mix_g, norm_ffn_g, final_norm_g, conv_w_in, conv_w, conv_w_out, ssm_w_in, ssm_a_re, ssm_a_im, ssm_log_dt, ssm_b_re, ssm_b_im, ssm_c_re, ssm_c_im, ssm_d, ssm_w_glu, attn_w_qkv, attn_lambda, attn_subln_g, attn_w_o, moe_router, moe_w_gate, moe_w_up, moe_w_down):
    B, L, D = x.shape
    Lc = ctx.shape[1]
    depth = ada_w.shape[0]
    geo = Geo(B, L, Lc, D)
    xc = jnp.concatenate([x.reshape(B * L, D), ctx.reshape(B * Lc, D),
                          jnp.zeros((geo.R - geo.Rv, D), F32)], axis=0)
    c_all = jnp.zeros((SUBLANES, D), F32).at[:B].set(c).at[B].set(c_ctx)
    modp = ada_modulation(c_all, ada_w, ada_b).reshape(depth, SUBLANES, 6, 1, D)
    ctx_live = [any(MIXER_READS_CTX[j % N_MIXERS] for j in range(i + 1, depth)) for i in range(depth)]
    for i in range(depth):
        kind, slot = i % N_MIXERS, i // N_MIXERS
        ctx_out = ctx_live[i]
        if kind == 0:
            xc = conv_mixer(geo, xc, modp, i, norm_mix_g[i], conv_w_in[slot], conv_w[slot],
                            conv_w_out[slot], ctx_out)
        elif kind == 1:
            xc = s5_mixer(geo, xc, modp, i, norm_mix_g[i], ssm_w_in[slot], ssm_a_re[slot], ssm_a_im[slot],
                          ssm_log_dt[slot], ssm_b_re[slot], ssm_b_im[slot], ssm_c_re[slot], ssm_c_im[slot],
                          ssm_d[slot], ssm_w_glu[slot])
        else:
            xc = attn_mixer(geo, xc, modp, i, norm_mix_g[i], attn_w_qkv[slot], attn_lambda[slot],
                            attn_subln_g[slot], attn_w_o[slot])
        xc = moe_layer(geo, xc, modp, i, norm_ffn_g[i], moe_router[i], moe_w_gate, moe_w_up, moe_w_down,
                       ctx_out)
    return final_norm(geo, xc, final_norm_g).reshape(B, L, D)
```
